```python
import math
import jax
import jax.numpy as jnp
from jax import lax
import numpy as np

D_MODEL = 2048
BATCH = 2
SEQ = 4096
DEPTH = 4

GRID_W = 64
CTX_LEN = 256
N_EVEN = (DEPTH + 1) // 2
N_ODD = DEPTH // 2
CHUNK = 64
CONV_K = 5
EPS = 1e-6
F32 = jnp.float32

GDN_HEADS = 16
GDN_DK = 128
GDN_DV = 128
GDN_QK_DIM = GDN_HEADS * GDN_DK
GDN_V_DIM = GDN_HEADS * GDN_DV
SSD_HEADS = 32
SSD_HEADDIM = 64
SSD_INNER = SSD_HEADS * SSD_HEADDIM
SSD_GROUPS = 4
SSD_STATE = 128
SSD_XBC = SSD_INNER + 2 * SSD_GROUPS * SSD_STATE
EVEN_SPLITS = (2 * GDN_QK_DIM + GDN_V_DIM, GDN_V_DIM, 4 * GDN_HEADS, SSD_INNER, SSD_XBC, 2 * SSD_HEADS)
EVEN_IN = sum(EVEN_SPLITS)
EVEN_MIX = GDN_V_DIM + SSD_INNER
GLA_HEADS = 4
GLA_DK = D_MODEL // 2 // GLA_HEADS
GLA_DV = D_MODEL // GLA_HEADS
GLA_GATE_RANK = 16
GLA_GATE_NORMALIZER = 16.0
ODD_SPLITS = (GLA_HEADS * GLA_DK, GLA_HEADS * GLA_DK, GLA_HEADS * GLA_DV, GLA_HEADS * GLA_DV, 2 * GLA_GATE_RANK)
ODD_IN = sum(ODD_SPLITS)
ODD_MIX = GLA_HEADS * GLA_DV
PEER_HEADS = 8
PEER_NKEYS = 128
PEER_EXPERTS = PEER_NKEYS * PEER_NKEYS
PEER_DKEY = 256
PEER_TOPK = 16
PEER_BLOCK = 128

kernel_name = 'hybrid_gdn_ssd_gla_peer_dit'


def rms_normalize(xf):
    return xf * lax.rsqrt(jnp.mean(jnp.square(xf), axis=-1, keepdims=True) + EPS)


def rmsnorm(x, w):
    return (rms_normalize(x.astype(F32)) * w.astype(F32)).astype(x.dtype)


def l2norm(x):
    xf = x.astype(F32)
    return xf * lax.rsqrt(jnp.sum(jnp.square(xf), axis=-1, keepdims=True) + EPS)


def split_cols(p, sizes):
    cuts = [int(s) for s in np.cumsum(sizes)[:-1]]
    return jnp.split(p, cuts, axis=-1)


def dwconv(x, w):
    ch = x.shape[-1]
    return lax.conv_general_dilated(x, w[:, None, :].astype(x.dtype), (1,), [(CONV_K // 2, CONV_K // 2)],
                                    dimension_numbers=('NWC', 'WIO', 'NWC'), feature_group_count=ch)


def to_col_major(h):
    b, s, d = h.shape
    rows = s // GRID_W
    return h.reshape(b, rows, GRID_W, d).transpose(0, 2, 1, 3).reshape(b, s, d)


def to_row_major(h):
    b, s, d = h.shape
    rows = s // GRID_W
    return h.reshape(b, GRID_W, rows, d).transpose(0, 2, 1, 3).reshape(b, s, d)


def tri_masks():
    i = jnp.arange(CHUNK)
    return i[:, None] >= i[None, :], i[:, None] > i[None, :]


def chunk_heads(t):
    b, l, h, d = t.shape
    return t.reshape(b, l // CHUNK, CHUNK, h, d).transpose(0, 3, 1, 2, 4)


def chunk_scalar(t):
    b, l, h = t.shape
    return t.reshape(b, l // CHUNK, CHUNK, h).transpose(0, 3, 1, 2)


def unchunk_heads(o):
    n, b, h, c, d = o.shape
    return o.transpose(1, 0, 3, 2, 4).reshape(b, n * c, h, d)


def chunks_first(ts):
    return tuple(jnp.moveaxis(t, 2, 0) for t in ts)


def gdn_core(q, k, v, g, beta, s0):
    q, k, v = (chunk_heads(t.astype(F32)) for t in (q, k, v))
    g = jnp.cumsum(chunk_scalar(g.astype(F32)), axis=-1)
    beta = chunk_scalar(beta.astype(F32))
    incl, strict = tri_masks()
    decay = jnp.where(incl, jnp.exp(jnp.where(incl, g[..., :, None] - g[..., None, :], 0.0)), 0.0)
    kb = k * beta[..., None]
    a = jnp.where(strict, jnp.einsum('bhncd,bhnsd->bhncs', kb, k) * decay, 0.0) + jnp.eye(CHUNK, dtype=F32)
    eg = jnp.exp(g)
    w = lax.linalg.triangular_solve(a, kb * eg[..., None], left_side=True, lower=True, unit_diagonal=True)
    u = lax.linalg.triangular_solve(a, v * beta[..., None], left_side=True, lower=True, unit_diagonal=True)
    attn = jnp.einsum('bhncd,bhnsd->bhncs', q, k) * decay
    g_last = g[..., -1]
    xs = chunks_first((q * eg[..., None], w, u, attn, k * jnp.exp(g_last[..., None] - g)[..., None], jnp.exp(g_last)))

    def step(S, inp):
        q_c, w_c, u_c, a_c, k_c, dec = inp
        v_new = u_c - jnp.einsum('bhcd,bhde->bhce', w_c, S)
        o = jnp.einsum('bhcd,bhde->bhce', q_c, S) + jnp.einsum('bhcs,bhse->bhce', a_c, v_new)
        S = S * dec[..., None, None] + jnp.einsum('bhcd,bhce->bhde', k_c, v_new)
        return S, o

    S, o = lax.scan(step, s0, xs)
    return unchunk_heads(o), S


def ssd_core(x, dt, dA, bm, cm, s0):
    bsz, l, h, p = x.shape
    hpg = h // SSD_GROUPS
    xd = chunk_heads((x.astype(F32) * dt.astype(F32)[..., None]))
    acs = jnp.cumsum(chunk_scalar(dA.astype(F32)), axis=-1)
    bm = chunk_heads(bm.astype(F32))
    cm = chunk_heads(cm.astype(F32))
    incl, _ = tri_masks()
    seg = jnp.where(incl, jnp.exp(jnp.where(incl, acs[..., :, None] - acs[..., None, :], 0.0)), 0.0)
    cb = jnp.einsum('bgncd,bgnsd->bgncs', cm, bm)
    scores = jnp.repeat(cb, hpg, axis=1) * seg
    y_diag = jnp.einsum('bhncs,bhnsp->bhncp', scores, xd)
    acs_last = acs[..., -1]
    xs = chunks_first((cm, bm, xd * jnp.exp(acs_last[..., None] - acs)[..., None], jnp.exp(acs), jnp.exp(acs_last), y_diag))

    def step(S, inp):
        c_c, b_c, xw_c, dout, dec, yd = inp
        sg = S.reshape(bsz, SSD_GROUPS, hpg, p, SSD_STATE)
        y_off = jnp.einsum('bgcn,bgkpn->bgkcp', c_c, sg).reshape(bsz, h, CHUNK, p) * dout[..., None]
        xg = xw_c.reshape(bsz, SSD_GROUPS, hpg, CHUNK, p)
        S = S * dec[..., None, None] + jnp.einsum('bgcn,bgkcp->bgkpn', b_c, xg).reshape(bsz, h, p, SSD_STATE)
        return S, yd + y_off

    S, y = lax.scan(step, s0, xs)
    return unchunk_heads(y), S


def gla_core(q, k, v, lg, s0):
    q, k, v, lg = (chunk_heads(t.astype(F32)) for t in (q, k, v, lg))
    b = jnp.cumsum(lg, axis=3)
    q_t = q * jnp.exp(b)
    k_t = k * jnp.exp(-b)
    incl, _ = tri_masks()
    attn = jnp.where(incl, jnp.einsum('bhncd,bhnsd->bhncs', q_t, k_t), 0.0)
    o_intra = jnp.einsum('bhncs,bhnse->bhnce', attn, v)
    b_last = b[..., -1, :]
    xs = chunks_first((q_t, k * jnp.exp(b_last[..., None, :] - b), v, jnp.exp(b_last), o_intra))

    def step(S, inp):
        q_c, k_c, v_c, dec, oi = inp
        o = oi + jnp.einsum('bhcd,bhde->bhce', q_c, S)
        S = S * dec[..., :, None] + jnp.einsum('bhcd,bhce->bhde', k_c, v_c)
        return S, o

    S, o = lax.scan(step, s0, xs)
    return unchunk_heads(o), S


def bidirectional(core, ctx_fwd, lat_fwd, ctx_bwd, lat_bwd, s0, need_ctx):
    rev = lambda ts: tuple(t[:, ::-1] for t in ts)
    o_cf, s_f = core(*ctx_fwd, s0)
    o_lf, _ = core(*lat_fwd, s_f)
    o_cb, s_b = core(*rev(ctx_bwd), s0)
    o_lb, _ = core(*rev(lat_bwd), s_b)
    o_ctx = o_cf + o_cb[:, ::-1] if need_ctx else None
    return o_ctx, o_lf + o_lb[:, ::-1]


def even_mixer(hc, hl, w_in, gdn_conv_w, gdn_a_log, gdn_dt_bias, gdn_norm_w, ssd_conv_w, ssd_conv_b,
               ssd_a_log, ssd_dt_bias, ssd_d, ssd_norm_w, w_out, need_ctx):
    def prep(h):
        bsz, l, _ = h.shape
        qkv, z_a, gates, z_b, xbc, dt_raw = split_cols(h @ w_in, EVEN_SPLITS)
        qkv = jax.nn.silu(dwconv(qkv, gdn_conv_w))
        q, k, v = split_cols(qkv, (GDN_QK_DIM, GDN_QK_DIM, GDN_V_DIM))
        q = l2norm(q.reshape(bsz, l, GDN_HEADS, GDN_DK)) * GDN_DK ** -0.5
        k = l2norm(k.reshape(bsz, l, GDN_HEADS, GDN_DK))
        v = v.reshape(bsz, l, GDN_HEADS, GDN_DV)
        gates = gates.astype(F32).reshape(bsz, l, 4, GDN_HEADS)
        beta = jax.nn.sigmoid(gates[:, :, 0:2])
        g = -jnp.exp(gdn_a_log.astype(F32)) * jax.nn.softplus(gates[:, :, 2:4] + gdn_dt_bias.astype(F32))
        gdn_f = (q, k, v, g[:, :, 0], beta[:, :, 0])
        gdn_b = (q, k, v, g[:, :, 1], beta[:, :, 1])
        xbc = jax.nn.silu(dwconv(xbc, ssd_conv_w) + ssd_conv_b)
        xs, bm, cm = split_cols(xbc, (SSD_INNER, SSD_GROUPS * SSD_STATE, SSD_GROUPS * SSD_STATE))
        xs = xs.reshape(bsz, l, SSD_HEADS, SSD_HEADDIM)
        bm = bm.reshape(bsz, l, SSD_GROUPS, SSD_STATE)
        cm = cm.reshape(bsz, l, SSD_GROUPS, SSD_STATE)
        dt = jax.nn.softplus(dt_raw.astype(F32).reshape(bsz, l, 2, SSD_HEADS) + ssd_dt_bias.astype(F32))
        dA = -dt * jnp.exp(ssd_a_log.astype(F32))
        ssd_f = (xs, dt[:, :, 0], dA[:, :, 0], bm, cm)
        ssd_b = (xs, dt[:, :, 1], dA[:, :, 1], bm, cm)
        return gdn_f, gdn_b, ssd_f, ssd_b, (z_a, z_b, xs)

    def finish(o_gdn, o_ssd, z_a, z_b, xs, dtype):
        bsz, l = o_gdn.shape[:2]
        y_a = rms_normalize(o_gdn) * gdn_norm_w.astype(F32) * jax.nn.silu(z_a.astype(F32).reshape(bsz, l, GDN_HEADS, GDN_DV))
        y_b = (o_ssd + ssd_d.astype(F32)[:, None] * xs.astype(F32)).reshape(bsz, l, SSD_INNER) * jax.nn.silu(z_b.astype(F32))
        y_b = rms_normalize(y_b.reshape(bsz, l, SSD_GROUPS, -1)).reshape(bsz, l, SSD_INNER) * ssd_norm_w.astype(F32)
        y = jnp.concatenate([y_a.reshape(bsz, l, GDN_V_DIM), y_b], axis=-1)
        return y.astype(dtype) @ w_out

    pc, pl = prep(hc), prep(hl)
    bsz = hl.shape[0]
    o_gc, o_gl = bidirectional(gdn_core, pc[0], pl[0], pc[1], pl[1],
                               jnp.zeros((bsz, GDN_HEADS, GDN_DK, GDN_DV), F32), need_ctx)
    o_sc, o_sl = bidirectional(ssd_core, pc[2], pl[2], pc[3], pl[3],
                               jnp.zeros((bsz, SSD_HEADS, SSD_HEADDIM, SSD_STATE), F32), need_ctx)
    y_l = finish(o_gl, o_sl, *pl[4], hl.dtype)
    y_c = finish(o_gc, o_sc, *pc[4], hc.dtype) if need_ctx else None
    return y_c, y_l


def odd_mixer(hc, hl, w_in, gla_w_gate, gla_b_gate, gla_norm_w, w_out, need_ctx):
    def prep(h):
        bsz, l, _ = h.shape
        q, k, v, r, lr = split_cols(h @ w_in, ODD_SPLITS)
        q = q.reshape(bsz, l, GLA_HEADS, GLA_DK) * GLA_DK ** -0.5
        k = k.reshape(bsz, l, GLA_HEADS, GLA_DK)
        v = v.reshape(bsz, l, GLA_HEADS, GLA_DV)
        lr = lr.astype(F32).reshape(bsz, l, 2, GLA_GATE_RANK)
        logits = jnp.einsum('bldr,drk->bldk', lr, gla_w_gate.astype(F32)) + gla_b_gate.astype(F32)
        lg = (jax.nn.log_sigmoid(logits) / GLA_GATE_NORMALIZER).reshape(bsz, l, 2, GLA_HEADS, GLA_DK)
        return (q, k, v, lg[:, :, 0]), (q, k, v, lg[:, :, 1]), r

    def finish(o, r, dtype):
        bsz, l = o.shape[:2]
        y = rms_normalize(o) * gla_norm_w.astype(F32) * jax.nn.silu(r.astype(F32).reshape(bsz, l, GLA_HEADS, GLA_DV))
        return y.reshape(bsz, l, ODD_MIX).astype(dtype) @ w_out

    pc, pl = prep(hc), prep(hl)
    bsz = hl.shape[0]
    o_c, o_l = bidirectional(gla_core, pc[0], pl[0], pc[1], pl[1],
                             jnp.zeros((bsz, GLA_HEADS, GLA_DK, GLA_DV), F32), need_ctx)
    y_l = finish(o_l, pl[2], hl.dtype)
    y_c = finish(o_c, pc[2], hc.dtype) if need_ctx else None
    return y_c, y_l


def peer_ffn(h, w_q, subkeys, u, v):
    bsz, l, d = h.shape
    xb = h.reshape(bsz * l // PEER_BLOCK, PEER_BLOCK, d)

    def block(xt):
        q = (xt @ w_q).reshape(PEER_BLOCK, PEER_HEADS, 2, PEER_DKEY // 2)
        s = jnp.einsum('thpd,hpkd->thpk', q, subkeys).astype(F32)
        top_s, top_i = lax.top_k(s, PEER_TOPK)
        cand_s = (top_s[:, :, 0, :, None] + top_s[:, :, 1, None, :]).reshape(PEER_BLOCK, PEER_HEADS, -1)
        cand_i = (top_i[:, :, 0, :, None] * PEER_NKEYS + top_i[:, :, 1, None, :]).reshape(PEER_BLOCK, PEER_HEADS, -1)
        best_s, best_pos = lax.top_k(cand_s, PEER_TOPK)
        experts = jnp.take_along_axis(cand_i, best_pos, axis=-1)
        gate = jax.nn.softmax(best_s, axis=-1)
        u_sel = jnp.take(u, experts, axis=0)
        pre = jnp.einsum('td,thkd->thk', xt, u_sel).astype(F32)
        coef = (gate * jax.nn.gelu(pre, approximate=False)).astype(xt.dtype)
        v_sel = jnp.take(v, experts, axis=0)
        return jnp.einsum('thk,thkd->td', coef, v_sel)

    return lax.map(block, xb).reshape(bsz, l, d)


def _dt_bias(key, shape):
    dt = jnp.exp(jax.random.uniform(key, shape, F32, math.log(1e-3), math.log(1e-1)))
    return dt + jnp.log(-jnp.expm1(-dt))


def _a_log(key, shape):
    return jnp.log(jax.random.uniform(key, shape, F32, 1.0, 16.0))


def setup_inputs(seed: int = 0) -> dict:
    key = jax.random.key(seed)
    ks = iter(jax.random.split(key, 40))
    nrm = lambda shape, std: jax.random.normal(next(ks), shape, F32) * std
    gain = lambda shape: 1.0 + 0.02 * jax.random.normal(next(ks), shape, F32)
    return {
        'x': nrm((BATCH, SEQ, D_MODEL), 1.0),
        'c': nrm((BATCH, D_MODEL), 1.0),
        'ctx': nrm((BATCH, CTX_LEN, D_MODEL), 1.0),
        'c_ctx': nrm((D_MODEL,), 1.0),
        'ada_w': nrm((DEPTH, D_MODEL, 6 * D_MODEL), 0.5 * D_MODEL ** -0.5),
        'ada_b': nrm((DEPTH, 6 * D_MODEL), 0.02),
        'norm1_w': gain((DEPTH, D_MODEL)),
        'norm2_w': gain((DEPTH, D_MODEL)),
        'ev_w_in': nrm((N_EVEN, D_MODEL, EVEN_IN), D_MODEL ** -0.5),
        'gdn_conv_w': nrm((N_EVEN, CONV_K, 2 * GDN_QK_DIM + GDN_V_DIM), CONV_K ** -0.5),
        'gdn_a_log': _a_log(next(ks), (N_EVEN, 2, GDN_HEADS)),
        'gdn_dt_bias': _dt_bias(next(ks), (N_EVEN, 2, GDN_HEADS)),
        'gdn_norm_w': gain((N_EVEN, GDN_DV)),
        'ssd_conv_w': nrm((N_EVEN, CONV_K, SSD_XBC), CONV_K ** -0.5),
        'ssd_conv_b': nrm((N_EVEN, SSD_XBC), 0.02),
        'ssd_a_log': _a_log(next(ks), (N_EVEN, 2, SSD_HEADS)),
        'ssd_dt_bias': _dt_bias(next(ks), (N_EVEN, 2, SSD_HEADS)),
        'ssd_d': gain((N_EVEN, SSD_HEADS)),
        'ssd_norm_w': gain((N_EVEN, SSD_INNER)),
        'ev_w_out': nrm((N_EVEN, EVEN_MIX, D_MODEL), EVEN_MIX ** -0.5),
        'od_w_in': nrm((N_ODD, D_MODEL, ODD_IN), D_MODEL ** -0.5),
        'gla_w_gate': nrm((N_ODD, 2, GLA_GATE_RANK, GLA_HEADS * GLA_DK), GLA_GATE_RANK ** -0.5),
        'gla_b_gate': nrm((N_ODD, 2, GLA_HEADS * GLA_DK), 0.1),
        'gla_norm_w': gain((N_ODD, GLA_DV)),
        'od_w_out': nrm((N_ODD, ODD_MIX, D_MODEL), ODD_MIX ** -0.5),
        'peer_w_q': nrm((DEPTH, D_MODEL, PEER_HEADS * PEER_DKEY), D_MODEL ** -0.5),
        'peer_subkeys': nrm((DEPTH, PEER_HEADS, 2, PEER_NKEYS, PEER_DKEY // 2), (PEER_DKEY // 2) ** -0.5),
        'peer_u': nrm((DEPTH, PEER_EXPERTS, D_MODEL), D_MODEL ** -0.5),
        'peer_v': nrm((DEPTH, PEER_EXPERTS, D_MODEL), PEER_HEADS ** -0.5),
        'final_norm_w': gain((D_MODEL,)),
    }


def reference(x, c, ctx, c_ctx, ada_w, ada_b, norm1_w, norm2_w, ev_w_in, gdn_conv_w, gdn_a_log, gdn_dt_bias,
              gdn_norm_w, ssd_conv_w, ssd_conv_b, ssd_a_log, ssd_dt_bias, ssd_d, ssd_norm_w, ev_w_out,
              od_w_in, gla_w_gate, gla_b_gate, gla_norm_w, od_w_out, peer_w_q, peer_subkeys, peer_u, peer_v,
              final_norm_w):
    h, hc = x, ctx
    sc_lat = jax.nn.silu(c)
    sc_ctx = jax.nn.silu(c_ctx)
    for i in range(DEPTH):
        need_ctx = i < DEPTH - 1
        j = i // 2
        shift1, scale1, gate1, shift2, scale2, gate2 = (m[:, None, :] for m in jnp.split(sc_lat @ ada_w[i] + ada_b[i], 6, axis=-1))
        cshift1, cscale1, cgate1, cshift2, cscale2, cgate2 = jnp.split(sc_ctx @ ada_w[i] + ada_b[i], 6, axis=-1)
        a_l = rmsnorm(h, norm1_w[i]) * (1 + scale1) + shift1
        a_c = rmsnorm(hc, norm1_w[i]) * (1 + cscale1) + cshift1
        if i % 2 == 0:
            y_c, y_l = even_mixer(a_c, a_l, ev_w_in[j], gdn_conv_w[j], gdn_a_log[j], gdn_dt_bias[j], gdn_norm_w[j],
                                  ssd_conv_w[j], ssd_conv_b[j], ssd_a_log[j], ssd_dt_bias[j], ssd_d[j], ssd_norm_w[j],
                                  ev_w_out[j], need_ctx)
        else:
            y_c, y_l = odd_mixer(a_c, to_col_major(a_l), od_w_in[j], gla_w_gate[j], gla_b_gate[j], gla_norm_w[j],
                                 od_w_out[j], need_ctx)
            y_l = to_row_major(y_l)
        h = h + gate1 * y_l
        h = h + gate2 * peer_ffn(rmsnorm(h, norm2_w[i]) * (1 + scale2) + shift2,
                                 peer_w_q[i], peer_subkeys[i], peer_u[i], peer_v[i])
        if need_ctx:
            hc = hc + cgate1 * y_c
            hc = hc + cgate2 * peer_ffn(rmsnorm(hc, norm2_w[i]) * (1 + cscale2) + cshift2,
                                        peer_w_q[i], peer_subkeys[i], peer_u[i], peer_v[i])
    return rmsnorm(h, final_norm_w)
```

```python
import functools
import math
import jax
import jax.numpy as jnp
from jax import lax
import numpy as np
from jax.experimental import pallas as pl
from jax.experimental.pallas import tpu as pltpu

D_MODEL = 2048
BATCH = 2
SEQ = 4096
DEPTH = 4

GRID_W = 64
CTX_LEN = 256
N_EVEN = (DEPTH + 1) // 2
N_ODD = DEPTH // 2
CHUNK = 64
CONV_K = 5
EPS = 1e-6
F32 = jnp.float32
BF16 = jnp.bfloat16

GDN_HEADS = 16
GDN_DK = 128
GDN_DV = 128
GDN_QK_DIM = GDN_HEADS * GDN_DK
GDN_V_DIM = GDN_HEADS * GDN_DV
SSD_HEADS = 32
SSD_HEADDIM = 64
SSD_INNER = SSD_HEADS * SSD_HEADDIM
SSD_GROUPS = 4
SSD_STATE = 128
SSD_XBC = SSD_INNER + 2 * SSD_GROUPS * SSD_STATE
EVEN_SPLITS = (2 * GDN_QK_DIM + GDN_V_DIM, GDN_V_DIM, 4 * GDN_HEADS, SSD_INNER, SSD_XBC, 2 * SSD_HEADS)
EVEN_IN = sum(EVEN_SPLITS)
EVEN_MIX = GDN_V_DIM + SSD_INNER
GLA_HEADS = 4
GLA_DK = D_MODEL // 2 // GLA_HEADS
GLA_DV = D_MODEL // GLA_HEADS
GLA_GATE_RANK = 16
GLA_GATE_NORMALIZER = 16.0
ODD_SPLITS = (GLA_HEADS * GLA_DK, GLA_HEADS * GLA_DK, GLA_HEADS * GLA_DV, GLA_HEADS * GLA_DV, 2 * GLA_GATE_RANK)
ODD_IN = sum(ODD_SPLITS)
ODD_MIX = GLA_HEADS * GLA_DV
PEER_HEADS = 8
PEER_NKEYS = 128
PEER_EXPERTS = PEER_NKEYS * PEER_NKEYS
PEER_DKEY = 256
PEER_TOPK = 16
PEER_BLOCK = 128

VMEM_LIMIT = 48 * 1024 * 1024


def _mm_kernel(a_ref, b_ref, o_ref):
    o_ref[...] = jnp.dot(a_ref[...].astype(BF16), b_ref[...].astype(BF16),
                         preferred_element_type=F32).astype(o_ref.dtype)


def matmul(a, b, tm, tn, out_dtype=F32):
    m, k = a.shape
    _, n = b.shape
    assert m % tm == 0 and n % tn == 0, (m, n, tm, tn)
    return pl.pallas_call(
        _mm_kernel,
        grid=(n // tn, m // tm),
        in_specs=[pl.BlockSpec((tm, k), lambda j, i: (i, 0)),
                  pl.BlockSpec((k, tn), lambda j, i: (0, j))],
        out_specs=pl.BlockSpec((tm, tn), lambda j, i: (i, j)),
        out_shape=jax.ShapeDtypeStruct((m, n), out_dtype),
        compiler_params=pltpu.CompilerParams(dimension_semantics=("parallel", "parallel"),
                                             vmem_limit_bytes=VMEM_LIMIT),
    )(a, b)


def proj(h, w, tn):
    bsz, l, d = h.shape
    tm = 512 if (bsz * l) % 512 == 0 else 256
    return matmul(h.reshape(bsz * l, d), w, tm, tn).reshape(bsz, l, w.shape[1])


def rms_normalize(xf):
    return xf * lax.rsqrt(jnp.mean(jnp.square(xf), axis=-1, keepdims=True) + EPS)


def rmsnorm(x, w):
    return (rms_normalize(x.astype(F32)) * w.astype(F32)).astype(x.dtype)


def l2norm(x):
    xf = x.astype(F32)
    return xf * lax.rsqrt(jnp.sum(jnp.square(xf), axis=-1, keepdims=True) + EPS)


def split_cols(p, sizes):
    cuts = [int(s) for s in np.cumsum(sizes)[:-1]]
    return jnp.split(p, cuts, axis=-1)


def dwconv(x, w):
    ch = x.shape[-1]
    return lax.conv_general_dilated(x, w[:, None, :].astype(x.dtype), (1,), [(CONV_K // 2, CONV_K // 2)],
                                    dimension_numbers=('NWC', 'WIO', 'NWC'), feature_group_count=ch)


def to_col_major(h):
    b, s, d = h.shape
    rows = s // GRID_W
    return h.reshape(b, rows, GRID_W, d).transpose(0, 2, 1, 3).reshape(b, s, d)


def to_row_major(h):
    b, s, d = h.shape
    rows = s // GRID_W
    return h.reshape(b, GRID_W, rows, d).transpose(0, 2, 1, 3).reshape(b, s, d)


def tri_masks():
    i = jnp.arange(CHUNK)
    return i[:, None] >= i[None, :], i[:, None] > i[None, :]


def chunk_heads(t):
    b, l, h, d = t.shape
    return t.reshape(b, l // CHUNK, CHUNK, h, d).transpose(0, 3, 1, 2, 4)


def chunk_scalar(t):
    b, l, h = t.shape
    return t.reshape(b, l // CHUNK, CHUNK, h).transpose(0, 3, 1, 2)


def unchunk_heads(o):
    n, b, h, c, d = o.shape
    return o.transpose(1, 0, 3, 2, 4).reshape(b, n * c, h, d)


def chunks_first(ts):
    return tuple(jnp.moveaxis(t, 2, 0) for t in ts)


def gdn_core(q, k, v, g, beta, s0):
    q, k, v = (chunk_heads(t.astype(F32)) for t in (q, k, v))
    g = jnp.cumsum(chunk_scalar(g.astype(F32)), axis=-1)
    beta = chunk_scalar(beta.astype(F32))
    incl, strict = tri_masks()
    decay = jnp.where(incl, jnp.exp(jnp.where(incl, g[..., :, None] - g[..., None, :], 0.0)), 0.0)
    kb = k * beta[..., None]
    a = jnp.where(strict, jnp.einsum('bhncd,bhnsd->bhncs', kb, k) * decay, 0.0) + jnp.eye(CHUNK, dtype=F32)
    eg = jnp.exp(g)
    w = lax.linalg.triangular_solve(a, kb * eg[..., None], left_side=True, lower=True, unit_diagonal=True)
    u = lax.linalg.triangular_solve(a, v * beta[..., None], left_side=True, lower=True, unit_diagonal=True)
    attn = jnp.einsum('bhncd,bhnsd->bhncs', q, k) * decay
    g_last = g[..., -1]
    xs = chunks_first((q * eg[..., None], w, u, attn, k * jnp.exp(g_last[..., None] - g)[..., None], jnp.exp(g_last)))

    def step(S, inp):
        q_c, w_c, u_c, a_c, k_c, dec = inp
        v_new = u_c - jnp.einsum('bhcd,bhde->bhce', w_c, S)
        o = jnp.einsum('bhcd,bhde->bhce', q_c, S) + jnp.einsum('bhcs,bhse->bhce', a_c, v_new)
        S = S * dec[..., None, None] + jnp.einsum('bhcd,bhce->bhde', k_c, v_new)
        return S, o

    S, o = lax.scan(step, s0, xs)
    return unchunk_heads(o), S


def ssd_core(x, dt, dA, bm, cm, s0):
    bsz, l, h, p = x.shape
    hpg = h // SSD_GROUPS
    xd = chunk_heads((x.astype(F32) * dt.astype(F32)[..., None]))
    acs = jnp.cumsum(chunk_scalar(dA.astype(F32)), axis=-1)
    bm = chunk_heads(bm.astype(F32))
    cm = chunk_heads(cm.astype(F32))
    incl, _ = tri_masks()
    seg = jnp.where(incl, jnp.exp(jnp.where(incl, acs[..., :, None] - acs[..., None, :], 0.0)), 0.0)
    cb = jnp.einsum('bgncd,bgnsd->bgncs', cm, bm)
    scores = jnp.repeat(cb, hpg, axis=1) * seg
    y_diag = jnp.einsum('bhncs,bhnsp->bhncp', scores, xd)
    acs_last = acs[..., -1]
    xs = chunks_first((cm, bm, xd * jnp.exp(acs_last[..., None] - acs)[..., None], jnp.exp(acs), jnp.exp(acs_last), y_diag))

    def step(S, inp):
        c_c, b_c, xw_c, dout, dec, yd = inp
        sg = S.reshape(bsz, SSD_GROUPS, hpg, p, SSD_STATE)
        y_off = jnp.einsum('bgcn,bgkpn->bgkcp', c_c, sg).reshape(bsz, h, CHUNK, p) * dout[..., None]
        xg = xw_c.reshape(bsz, SSD_GROUPS, hpg, CHUNK, p)
        S = S * dec[..., None, None] + jnp.einsum('bgcn,bgkcp->bgkpn', b_c, xg).reshape(bsz, h, p, SSD_STATE)
        return S, yd + y_off

    S, y = lax.scan(step, s0, xs)
    return unchunk_heads(y), S


def gla_core(q, k, v, lg, s0):
    q, k, v, lg = (chunk_heads(t.astype(F32)) for t in (q, k, v, lg))
    b = jnp.cumsum(lg, axis=3)
    q_t = q * jnp.exp(b)
    k_t = k * jnp.exp(-b)
    incl, _ = tri_masks()
    attn = jnp.where(incl, jnp.einsum('bhncd,bhnsd->bhncs', q_t, k_t), 0.0)
    o_intra = jnp.einsum('bhncs,bhnse->bhnce', attn, v)
    b_last = b[..., -1, :]
    xs = chunks_first((q_t, k * jnp.exp(b_last[..., None, :] - b), v, jnp.exp(b_last), o_intra))

    def step(S, inp):
        q_c, k_c, v_c, dec, oi = inp
        o = oi + jnp.einsum('bhcd,bhde->bhce', q_c, S)
        S = S * dec[..., :, None] + jnp.einsum('bhcd,bhce->bhde', k_c, v_c)
        return S, o

    S, o = lax.scan(step, s0, xs)
    return unchunk_heads(o), S


def bidirectional(core, ctx_fwd, lat_fwd, ctx_bwd, lat_bwd, s0, need_ctx):
    rev = lambda ts: tuple(t[:, ::-1] for t in ts)
    o_cf, s_f = core(*ctx_fwd, s0)
    o_lf, _ = core(*lat_fwd, s_f)
    o_cb, s_b = core(*rev(ctx_bwd), s0)
    o_lb, _ = core(*rev(lat_bwd), s_b)
    o_ctx = o_cf + o_cb[:, ::-1] if need_ctx else None
    return o_ctx, o_lf + o_lb[:, ::-1]


def even_mixer(hc, hl, w_in, gdn_conv_w, gdn_a_log, gdn_dt_bias, gdn_norm_w, ssd_conv_w, ssd_conv_b,
               ssd_a_log, ssd_dt_bias, ssd_d, ssd_norm_w, w_out, need_ctx):
    def prep(h):
        bsz, l, _ = h.shape
        qkv, z_a, gates, z_b, xbc, dt_raw = split_cols(proj(h, w_in, 640), EVEN_SPLITS)
        qkv = jax.nn.silu(dwconv(qkv, gdn_conv_w))
        q, k, v = split_cols(qkv, (GDN_QK_DIM, GDN_QK_DIM, GDN_V_DIM))
        q = l2norm(q.reshape(bsz, l, GDN_HEADS, GDN_DK)) * GDN_DK ** -0.5
        k = l2norm(k.reshape(bsz, l, GDN_HEADS, GDN_DK))
        v = v.reshape(bsz, l, GDN_HEADS, GDN_DV)
        gates = gates.astype(F32).reshape(bsz, l, 4, GDN_HEADS)
        beta = jax.nn.sigmoid(gates[:, :, 0:2])
        g = -jnp.exp(gdn_a_log.astype(F32)) * jax.nn.softplus(gates[:, :, 2:4] + gdn_dt_bias.astype(F32))
        gdn_f = (q, k, v, g[:, :, 0], beta[:, :, 0])
        gdn_b = (q, k, v, g[:, :, 1], beta[:, :, 1])
        xbc = jax.nn.silu(dwconv(xbc, ssd_conv_w) + ssd_conv_b)
        xs, bm, cm = split_cols(xbc, (SSD_INNER, SSD_GROUPS * SSD_STATE, SSD_GROUPS * SSD_STATE))
        xs = xs.reshape(bsz, l, SSD_HEADS, SSD_HEADDIM)
        bm = bm.reshape(bsz, l, SSD_GROUPS, SSD_STATE)
        cm = cm.reshape(bsz, l, SSD_GROUPS, SSD_STATE)
        dt = jax.nn.softplus(dt_raw.astype(F32).reshape(bsz, l, 2, SSD_HEADS) + ssd_dt_bias.astype(F32))
        dA = -dt * jnp.exp(ssd_a_log.astype(F32))
        ssd_f = (xs, dt[:, :, 0], dA[:, :, 0], bm, cm)
        ssd_b = (xs, dt[:, :, 1], dA[:, :, 1], bm, cm)
        return gdn_f, gdn_b, ssd_f, ssd_b, (z_a, z_b, xs)

    def finish(o_gdn, o_ssd, z_a, z_b, xs, dtype):
        bsz, l = o_gdn.shape[:2]
        y_a = rms_normalize(o_gdn) * gdn_norm_w.astype(F32) * jax.nn.silu(z_a.astype(F32).reshape(bsz, l, GDN_HEADS, GDN_DV))
        y_b = (o_ssd + ssd_d.astype(F32)[:, None] * xs.astype(F32)).reshape(bsz, l, SSD_INNER) * jax.nn.silu(z_b.astype(F32))
        y_b = rms_normalize(y_b.reshape(bsz, l, SSD_GROUPS, -1)).reshape(bsz, l, SSD_INNER) * ssd_norm_w.astype(F32)
        y = jnp.concatenate([y_a.reshape(bsz, l, GDN_V_DIM), y_b], axis=-1)
        return proj(y.astype(dtype), w_out, 512)

    pc, pl_ = prep(hc), prep(hl)
    bsz = hl.shape[0]
    o_gc, o_gl = bidirectional(gdn_core, pc[0], pl_[0], pc[1], pl_[1],
                               jnp.zeros((bsz, GDN_HEADS, GDN_DK, GDN_DV), F32), need_ctx)
    o_sc, o_sl = bidirectional(ssd_core, pc[2], pl_[2], pc[3], pl_[3],
                               jnp.zeros((bsz, SSD_HEADS, SSD_HEADDIM, SSD_STATE), F32), need_ctx)
    y_l = finish(o_gl, o_sl, *pl_[4], hl.dtype)
    y_c = finish(o_gc, o_sc, *pc[4], hc.dtype) if need_ctx else None
    return y_c, y_l


def odd_mixer(hc, hl, w_in, gla_w_gate, gla_b_gate, gla_norm_w, w_out, need_ctx):
    w_in_p = jnp.pad(w_in, ((0, 0), (0, 6272 - ODD_IN)))

    def prep(h):
        bsz, l, _ = h.shape
        q, k, v, r, lr = split_cols(proj(h, w_in_p, 896)[..., :ODD_IN], ODD_SPLITS)
        q = q.reshape(bsz, l, GLA_HEADS, GLA_DK) * GLA_DK ** -0.5
        k = k.reshape(bsz, l, GLA_HEADS, GLA_DK)
        v = v.reshape(bsz, l, GLA_HEADS, GLA_DV)
        lr = lr.astype(F32).reshape(bsz, l, 2, GLA_GATE_RANK)
        logits = jnp.einsum('bldr,drk->bldk', lr, gla_w_gate.astype(F32)) + gla_b_gate.astype(F32)
        lg = (jax.nn.log_sigmoid(logits) / GLA_GATE_NORMALIZER).reshape(bsz, l, 2, GLA_HEADS, GLA_DK)
        return (q, k, v, lg[:, :, 0]), (q, k, v, lg[:, :, 1]), r

    def finish(o, r, dtype):
        bsz, l = o.shape[:2]
        y = rms_normalize(o) * gla_norm_w.astype(F32) * jax.nn.silu(r.astype(F32).reshape(bsz, l, GLA_HEADS, GLA_DV))
        return proj(y.reshape(bsz, l, ODD_MIX).astype(dtype), w_out, 512)

    pc, pl_ = prep(hc), prep(hl)
    bsz = hl.shape[0]
    o_c, o_l = bidirectional(gla_core, pc[0], pl_[0], pc[1], pl_[1],
                             jnp.zeros((bsz, GLA_HEADS, GLA_DK, GLA_DV), F32), need_ctx)
    y_l = finish(o_l, pl_[2], hl.dtype)
    y_c = finish(o_c, pc[2], hc.dtype) if need_ctx else None
    return y_c, y_l


def peer_ffn(h, w_q, subkeys, u, v):
    bsz, l, d = h.shape
    xb = h.reshape(bsz * l // PEER_BLOCK, PEER_BLOCK, d)
    qall = proj(h, w_q, 512).reshape(bsz * l // PEER_BLOCK, PEER_BLOCK, PEER_HEADS * PEER_DKEY)

    def block(inp):
        xt, q = inp
        q = q.reshape(PEER_BLOCK, PEER_HEADS, 2, PEER_DKEY // 2)
        s = jnp.einsum('thpd,hpkd->thpk', q, subkeys).astype(F32)
        top_s, top_i = lax.top_k(s, PEER_TOPK)
        cand_s = (top_s[:, :, 0, :, None] + top_s[:, :, 1, None, :]).reshape(PEER_BLOCK, PEER_HEADS, -1)
        cand_i = (top_i[:, :, 0, :, None] * PEER_NKEYS + top_i[:, :, 1, None, :]).reshape(PEER_BLOCK, PEER_HEADS, -1)
        best_s, best_pos = lax.top_k(cand_s, PEER_TOPK)
        experts = jnp.take_along_axis(cand_i, best_pos, axis=-1)
        gate = jax.nn.softmax(best_s, axis=-1)
        u_sel = jnp.take(u, experts, axis=0)
        pre = jnp.einsum('td,thkd->thk', xt, u_sel).astype(F32)
        coef = (gate * jax.nn.gelu(pre, approximate=False)).astype(xt.dtype)
        v_sel = jnp.take(v, experts, axis=0)
        return jnp.einsum('thk,thkd->td', coef, v_sel)

    return lax.map(block, (xb, qall)).reshape(bsz, l, d)


def kernel(x, c, ctx, c_ctx, ada_w, ada_b, norm1_w, norm2_w, ev_w_in, gdn_conv_w, gdn_a_log, gdn_dt_bias,
           gdn_norm_w, ssd_conv_w, ssd_conv_b, ssd_a_log, ssd_dt_bias, ssd_d, ssd_norm_w, ev_w_out,
           od_w_in, gla_w_gate, gla_b_gate, gla_norm_w, od_w_out, peer_w_q, peer_subkeys, peer_u, peer_v,
           final_norm_w):
    h, hc = x, ctx
    sc_lat = jax.nn.silu(c)
    sc_ctx = jax.nn.silu(c_ctx)
    for i in range(DEPTH):
        need_ctx = i < DEPTH - 1
        j = i // 2
        shift1, scale1, gate1, shift2, scale2, gate2 = (m[:, None, :] for m in jnp.split(sc_lat @ ada_w[i] + ada_b[i], 6, axis=-1))
        cshift1, cscale1, cgate1, cshift2, cscale2, cgate2 = jnp.split(sc_ctx @ ada_w[i] + ada_b[i], 6, axis=-1)
        a_l = rmsnorm(h, norm1_w[i]) * (1 + scale1) + shift1
        a_c = rmsnorm(hc, norm1_w[i]) * (1 + cscale1) + cshift1
        if i % 2 == 0:
            y_c, y_l = even_mixer(a_c, a_l, ev_w_in[j], gdn_conv_w[j], gdn_a_log[j], gdn_dt_bias[j], gdn_norm_w[j],
                                  ssd_conv_w[j], ssd_conv_b[j], ssd_a_log[j], ssd_dt_bias[j], ssd_d[j], ssd_norm_w[j],
                                  ev_w_out[j], need_ctx)
        else:
            y_c, y_l = odd_mixer(a_c, to_col_major(a_l), od_w_in[j], gla_w_gate[j], gla_b_gate[j], gla_norm_w[j],
                                 od_w_out[j], need_ctx)
            y_l = to_row_major(y_l)
        h = h + gate1 * y_l
        h = h + gate2 * peer_ffn(rmsnorm(h, norm2_w[i]) * (1 + scale2) + shift2,
                                 peer_w_q[i], peer_subkeys[i], peer_u[i], peer_v[i])
        if need_ctx:
            hc = hc + cgate1 * y_c
            hc = hc + cgate2 * peer_ffn(rmsnorm(hc, norm2_w[i]) * (1 + cscale2) + cshift2,
                                        peer_w_q[i], peer_subkeys[i], peer_u[i], peer_v[i])
    return rmsnorm(h, final_norm_w)
```

```python
import functools
import math
import jax
import jax.numpy as jnp
from jax import lax
import numpy as np
from jax.experimental import pallas as pl
from jax.experimental.pallas import tpu as pltpu

D_MODEL = 2048
BATCH = 2
SEQ = 4096
DEPTH = 4

GRID_W = 64
CTX_LEN = 256
N_EVEN = (DEPTH + 1) // 2
N_ODD = DEPTH // 2
CHUNK = 64
CONV_K = 5
EPS = 1e-6
F32 = jnp.float32
BF16 = jnp.bfloat16

GDN_HEADS = 16
GDN_DK = 128
GDN_DV = 128
GDN_QK_DIM = GDN_HEADS * GDN_DK
GDN_V_DIM = GDN_HEADS * GDN_DV
SSD_HEADS = 32
SSD_HEADDIM = 64
SSD_INNER = SSD_HEADS * SSD_HEADDIM
SSD_GROUPS = 4
SSD_STATE = 128
SSD_XBC = SSD_INNER + 2 * SSD_GROUPS * SSD_STATE
EVEN_SPLITS = (2 * GDN_QK_DIM + GDN_V_DIM, GDN_V_DIM, 4 * GDN_HEADS, SSD_INNER, SSD_XBC, 2 * SSD_HEADS)
EVEN_IN = sum(EVEN_SPLITS)
EVEN_MIX = GDN_V_DIM + SSD_INNER
GLA_HEADS = 4
GLA_DK = D_MODEL // 2 // GLA_HEADS
GLA_DV = D_MODEL // GLA_HEADS
GLA_GATE_RANK = 16
GLA_GATE_NORMALIZER = 16.0
ODD_SPLITS = (GLA_HEADS * GLA_DK, GLA_HEADS * GLA_DK, GLA_HEADS * GLA_DV, GLA_HEADS * GLA_DV, 2 * GLA_GATE_RANK)
ODD_IN = sum(ODD_SPLITS)
ODD_MIX = GLA_HEADS * GLA_DV
PEER_HEADS = 8
PEER_NKEYS = 128
PEER_EXPERTS = PEER_NKEYS * PEER_NKEYS
PEER_DKEY = 256
PEER_TOPK = 16
PEER_BLOCK = 128

VMEM_LIMIT = 48 * 1024 * 1024


def _mm_kernel(a_ref, b_ref, o_ref):
    o_ref[...] = jnp.dot(a_ref[...].astype(BF16), b_ref[...].astype(BF16),
                         preferred_element_type=F32).astype(o_ref.dtype)


def matmul(a, b, tm, tn, out_dtype=F32):
    m, k = a.shape
    _, n = b.shape
    assert m % tm == 0 and n % tn == 0, (m, n, tm, tn)
    return pl.pallas_call(
        _mm_kernel,
        grid=(n // tn, m // tm),
        in_specs=[pl.BlockSpec((tm, k), lambda j, i: (i, 0)),
                  pl.BlockSpec((k, tn), lambda j, i: (0, j))],
        out_specs=pl.BlockSpec((tm, tn), lambda j, i: (i, j)),
        out_shape=jax.ShapeDtypeStruct((m, n), out_dtype),
        compiler_params=pltpu.CompilerParams(dimension_semantics=("parallel", "parallel"),
                                             vmem_limit_bytes=VMEM_LIMIT),
    )(a, b)


def proj(h, w, tn):
    bsz, l, d = h.shape
    tm = 512 if (bsz * l) % 512 == 0 else 256
    return matmul(h.reshape(bsz * l, d), w, tm, tn).reshape(bsz, l, w.shape[1])


def rms_normalize(xf):
    return xf * lax.rsqrt(jnp.mean(jnp.square(xf), axis=-1, keepdims=True) + EPS)


def rmsnorm(x, w):
    return (rms_normalize(x.astype(F32)) * w.astype(F32)).astype(x.dtype)


def l2norm(x):
    xf = x.astype(F32)
    return xf * lax.rsqrt(jnp.sum(jnp.square(xf), axis=-1, keepdims=True) + EPS)


def split_cols(p, sizes):
    cuts = [int(s) for s in np.cumsum(sizes)[:-1]]
    return jnp.split(p, cuts, axis=-1)


def dwconv(x, w):
    ch = x.shape[-1]
    return lax.conv_general_dilated(x, w[:, None, :].astype(x.dtype), (1,), [(CONV_K // 2, CONV_K // 2)],
                                    dimension_numbers=('NWC', 'WIO', 'NWC'), feature_group_count=ch)


def to_col_major(h):
    b, s, d = h.shape
    rows = s // GRID_W
    return h.reshape(b, rows, GRID_W, d).transpose(0, 2, 1, 3).reshape(b, s, d)


def to_row_major(h):
    b, s, d = h.shape
    rows = s // GRID_W
    return h.reshape(b, GRID_W, rows, d).transpose(0, 2, 1, 3).reshape(b, s, d)


def tri_masks():
    i = jnp.arange(CHUNK)
    return i[:, None] >= i[None, :], i[:, None] > i[None, :]


def chunk_heads(t):
    b, l, h, d = t.shape
    return t.reshape(b, l // CHUNK, CHUNK, h, d).transpose(0, 3, 1, 2, 4)


def chunk_scalar(t):
    b, l, h = t.shape
    return t.reshape(b, l // CHUNK, CHUNK, h).transpose(0, 3, 1, 2)


def unchunk_heads(o):
    n, b, h, c, d = o.shape
    return o.transpose(1, 0, 3, 2, 4).reshape(b, n * c, h, d)


def chunks_first(ts):
    return tuple(jnp.moveaxis(t, 2, 0) for t in ts)


def gdn_core(q, k, v, g, beta, s0):
    q, k, v = (chunk_heads(t.astype(F32)) for t in (q, k, v))
    g = jnp.cumsum(chunk_scalar(g.astype(F32)), axis=-1)
    beta = chunk_scalar(beta.astype(F32))
    incl, strict = tri_masks()
    decay = jnp.where(incl, jnp.exp(jnp.where(incl, g[..., :, None] - g[..., None, :], 0.0)), 0.0)
    kb = k * beta[..., None]
    a = jnp.where(strict, jnp.einsum('bhncd,bhnsd->bhncs', kb, k) * decay, 0.0) + jnp.eye(CHUNK, dtype=F32)
    eg = jnp.exp(g)
    w = lax.linalg.triangular_solve(a, kb * eg[..., None], left_side=True, lower=True, unit_diagonal=True)
    u = lax.linalg.triangular_solve(a, v * beta[..., None], left_side=True, lower=True, unit_diagonal=True)
    attn = jnp.einsum('bhncd,bhnsd->bhncs', q, k) * decay
    g_last = g[..., -1]
    xs = chunks_first((q * eg[..., None], w, u, attn, k * jnp.exp(g_last[..., None] - g)[..., None], jnp.exp(g_last)))

    def step(S, inp):
        q_c, w_c, u_c, a_c, k_c, dec = inp
        v_new = u_c - jnp.einsum('bhcd,bhde->bhce', w_c, S)
        o = jnp.einsum('bhcd,bhde->bhce', q_c, S) + jnp.einsum('bhcs,bhse->bhce', a_c, v_new)
        S = S * dec[..., None, None] + jnp.einsum('bhcd,bhce->bhde', k_c, v_new)
        return S, o

    S, o = lax.scan(step, s0, xs)
    return unchunk_heads(o), S


def ssd_core(x, dt, dA, bm, cm, s0):
    bsz, l, h, p = x.shape
    hpg = h // SSD_GROUPS
    xd = chunk_heads((x.astype(F32) * dt.astype(F32)[..., None]))
    acs = jnp.cumsum(chunk_scalar(dA.astype(F32)), axis=-1)
    bm = chunk_heads(bm.astype(F32))
    cm = chunk_heads(cm.astype(F32))
    incl, _ = tri_masks()
    seg = jnp.where(incl, jnp.exp(jnp.where(incl, acs[..., :, None] - acs[..., None, :], 0.0)), 0.0)
    cb = jnp.einsum('bgncd,bgnsd->bgncs', cm, bm)
    scores = jnp.repeat(cb, hpg, axis=1) * seg
    y_diag = jnp.einsum('bhncs,bhnsp->bhncp', scores, xd)
    acs_last = acs[..., -1]
    xs = chunks_first((cm, bm, xd * jnp.exp(acs_last[..., None] - acs)[..., None], jnp.exp(acs), jnp.exp(acs_last), y_diag))

    def step(S, inp):
        c_c, b_c, xw_c, dout, dec, yd = inp
        sg = S.reshape(bsz, SSD_GROUPS, hpg, p, SSD_STATE)
        y_off = jnp.einsum('bgcn,bgkpn->bgkcp', c_c, sg).reshape(bsz, h, CHUNK, p) * dout[..., None]
        xg = xw_c.reshape(bsz, SSD_GROUPS, hpg, CHUNK, p)
        S = S * dec[..., None, None] + jnp.einsum('bgcn,bgkcp->bgkpn', b_c, xg).reshape(bsz, h, p, SSD_STATE)
        return S, yd + y_off

    S, y = lax.scan(step, s0, xs)
    return unchunk_heads(y), S


def gla_core(q, k, v, lg, s0):
    q, k, v, lg = (chunk_heads(t.astype(F32)) for t in (q, k, v, lg))
    b = jnp.cumsum(lg, axis=3)
    q_t = q * jnp.exp(b)
    k_t = k * jnp.exp(-b)
    incl, _ = tri_masks()
    attn = jnp.where(incl, jnp.einsum('bhncd,bhnsd->bhncs', q_t, k_t), 0.0)
    o_intra = jnp.einsum('bhncs,bhnse->bhnce', attn, v)
    b_last = b[..., -1, :]
    xs = chunks_first((q_t, k * jnp.exp(b_last[..., None, :] - b), v, jnp.exp(b_last), o_intra))

    def step(S, inp):
        q_c, k_c, v_c, dec, oi = inp
        o = oi + jnp.einsum('bhcd,bhde->bhce', q_c, S)
        S = S * dec[..., :, None] + jnp.einsum('bhcd,bhce->bhde', k_c, v_c)
        return S, o

    S, o = lax.scan(step, s0, xs)
    return unchunk_heads(o), S


def bidirectional(core, ctx_fwd, lat_fwd, ctx_bwd, lat_bwd, s0, need_ctx):
    rev = lambda ts: tuple(t[:, ::-1] for t in ts)
    o_cf, s_f = core(*ctx_fwd, s0)
    o_lf, _ = core(*lat_fwd, s_f)
    o_cb, s_b = core(*rev(ctx_bwd), s0)
    o_lb, _ = core(*rev(lat_bwd), s_b)
    o_ctx = o_cf + o_cb[:, ::-1] if need_ctx else None
    return o_ctx, o_lf + o_lb[:, ::-1]


def even_mixer(hc, hl, w_in, gdn_conv_w, gdn_a_log, gdn_dt_bias, gdn_norm_w, ssd_conv_w, ssd_conv_b,
               ssd_a_log, ssd_dt_bias, ssd_d, ssd_norm_w, w_out, need_ctx):
    def prep(h):
        bsz, l, _ = h.shape
        qkv, z_a, gates, z_b, xbc, dt_raw = split_cols(proj(h, w_in, 640), EVEN_SPLITS)
        qkv = jax.nn.silu(dwconv(qkv, gdn_conv_w))
        q, k, v = split_cols(qkv, (GDN_QK_DIM, GDN_QK_DIM, GDN_V_DIM))
        q = l2norm(q.reshape(bsz, l, GDN_HEADS, GDN_DK)) * GDN_DK ** -0.5
        k = l2norm(k.reshape(bsz, l, GDN_HEADS, GDN_DK))
        v = v.reshape(bsz, l, GDN_HEADS, GDN_DV)
        gates = gates.astype(F32).reshape(bsz, l, 4, GDN_HEADS)
        beta = jax.nn.sigmoid(gates[:, :, 0:2])
        g = -jnp.exp(gdn_a_log.astype(F32)) * jax.nn.softplus(gates[:, :, 2:4] + gdn_dt_bias.astype(F32))
        gdn_f = (q, k, v, g[:, :, 0], beta[:, :, 0])
        gdn_b = (q, k, v, g[:, :, 1], beta[:, :, 1])
        xbc = jax.nn.silu(dwconv(xbc, ssd_conv_w) + ssd_conv_b)
        xs, bm, cm = split_cols(xbc, (SSD_INNER, SSD_GROUPS * SSD_STATE, SSD_GROUPS * SSD_STATE))
        xs = xs.reshape(bsz, l, SSD_HEADS, SSD_HEADDIM)
        bm = bm.reshape(bsz, l, SSD_GROUPS, SSD_STATE)
        cm = cm.reshape(bsz, l, SSD_GROUPS, SSD_STATE)
        dt = jax.nn.softplus(dt_raw.astype(F32).reshape(bsz, l, 2, SSD_HEADS) + ssd_dt_bias.astype(F32))
        dA = -dt * jnp.exp(ssd_a_log.astype(F32))
        ssd_f = (xs, dt[:, :, 0], dA[:, :, 0], bm, cm)
        ssd_b = (xs, dt[:, :, 1], dA[:, :, 1], bm, cm)
        return gdn_f, gdn_b, ssd_f, ssd_b, (z_a, z_b, xs)

    def finish(o_gdn, o_ssd, z_a, z_b, xs, dtype):
        bsz, l = o_gdn.shape[:2]
        y_a = rms_normalize(o_gdn) * gdn_norm_w.astype(F32) * jax.nn.silu(z_a.astype(F32).reshape(bsz, l, GDN_HEADS, GDN_DV))
        y_b = (o_ssd + ssd_d.astype(F32)[:, None] * xs.astype(F32)).reshape(bsz, l, SSD_INNER) * jax.nn.silu(z_b.astype(F32))
        y_b = rms_normalize(y_b.reshape(bsz, l, SSD_GROUPS, -1)).reshape(bsz, l, SSD_INNER) * ssd_norm_w.astype(F32)
        y = jnp.concatenate([y_a.reshape(bsz, l, GDN_V_DIM), y_b], axis=-1)
        return proj(y.astype(dtype), w_out, 512)

    pc, pl_ = prep(hc), prep(hl)
    bsz = hl.shape[0]
    o_gc, o_gl = bidirectional(gdn_core, pc[0], pl_[0], pc[1], pl_[1],
                               jnp.zeros((bsz, GDN_HEADS, GDN_DK, GDN_DV), F32), need_ctx)
    o_sc, o_sl = bidirectional(ssd_core, pc[2], pl_[2], pc[3], pl_[3],
                               jnp.zeros((bsz, SSD_HEADS, SSD_HEADDIM, SSD_STATE), F32), need_ctx)
    y_l = finish(o_gl, o_sl, *pl_[4], hl.dtype)
    y_c = finish(o_gc, o_sc, *pc[4], hc.dtype) if need_ctx else None
    return y_c, y_l


def odd_mixer(hc, hl, w_in, gla_w_gate, gla_b_gate, gla_norm_w, w_out, need_ctx):
    w_in_p = jnp.pad(w_in, ((0, 0), (0, 6272 - ODD_IN)))

    def prep(h):
        bsz, l, _ = h.shape
        q, k, v, r, lr = split_cols(proj(h, w_in_p, 896)[..., :ODD_IN], ODD_SPLITS)
        q = q.reshape(bsz, l, GLA_HEADS, GLA_DK) * GLA_DK ** -0.5
        k = k.reshape(bsz, l, GLA_HEADS, GLA_DK)
        v = v.reshape(bsz, l, GLA_HEADS, GLA_DV)
        lr = lr.astype(F32).reshape(bsz, l, 2, GLA_GATE_RANK)
        logits = jnp.einsum('bldr,drk->bldk', lr, gla_w_gate.astype(F32)) + gla_b_gate.astype(F32)
        lg = (jax.nn.log_sigmoid(logits) / GLA_GATE_NORMALIZER).reshape(bsz, l, 2, GLA_HEADS, GLA_DK)
        return (q, k, v, lg[:, :, 0]), (q, k, v, lg[:, :, 1]), r

    def finish(o, r, dtype):
        bsz, l = o.shape[:2]
        y = rms_normalize(o) * gla_norm_w.astype(F32) * jax.nn.silu(r.astype(F32).reshape(bsz, l, GLA_HEADS, GLA_DV))
        return proj(y.reshape(bsz, l, ODD_MIX).astype(dtype), w_out, 512)

    pc, pl_ = prep(hc), prep(hl)
    bsz = hl.shape[0]
    o_c, o_l = bidirectional(gla_core, pc[0], pl_[0], pc[1], pl_[1],
                             jnp.zeros((bsz, GLA_HEADS, GLA_DK, GLA_DV), F32), need_ctx)
    y_l = finish(o_l, pl_[2], hl.dtype)
    y_c = finish(o_c, pc[2], hc.dtype) if need_ctx else None
    return y_c, y_l


def _ce(r, i, l):
    hi = jnp.maximum(r[i], r[l])
    lo = jnp.minimum(r[i], r[l])
    r[i], r[l] = hi, lo


def _bitonic_sort_desc(r):
    n = len(r)
    k = 2
    while k <= n:
        j = k // 2
        while j >= 1:
            for i in range(n):
                l = i ^ j
                if l > i:
                    if (i & k) == 0:
                        _ce(r, i, l)
                    else:
                        _ce(r, l, i)
            j //= 2
        k *= 2
    return r


def _bitonic_merge_desc(r):
    n = len(r)
    j = n // 2
    while j >= 1:
        for i in range(n):
            l = i ^ j
            if l > i:
                _ce(r, i, l)
        j //= 2
    return r


_PEER_CANDS = [(a, b) for a in range(PEER_TOPK + 1) for b in range(PEER_TOPK + 1) if (a + 1) * (b + 1) <= PEER_TOPK + 1]


def _peer_score_kernel(xn_ref, wq_ref, sk_ref, s2_ref, e2z_ref, thr_ref, e1_ref, s1_scr):
    tm = xn_ref.shape[0]
    nk = PEER_NKEYS
    q = jnp.dot(xn_ref[...], wq_ref[...], preferred_element_type=F32).astype(BF16)
    sub = lax.broadcasted_iota(jnp.int32, (8, tm), 0)
    packed = [[None] * (PEER_TOPK + 1), [None] * (PEER_TOPK + 1)]
    for h in range(PEER_HEADS):
        for p in range(2):
            hp = 2 * h + p
            s = lax.dot_general(sk_ref[hp], q[:, hp * (PEER_DKEY // 2):(hp + 1) * (PEER_DKEY // 2)],
                                (((1,), (1,)), ((), ())), preferred_element_type=F32)
            if p == 0:
                s1_scr[h] = s
            else:
                s2_ref[h] = s
            r = _bitonic_sort_desc([s[8 * g:8 * g + 8, :] for g in range(nk // 8)])
            nxt = None
            for shift in (4, 2, 1):
                other = [pltpu.roll(r[PEER_TOPK - 1 - i], shift, 0) for i in range(PEER_TOPK)]
                drop = functools.reduce(jnp.maximum, [jnp.minimum(r[i], other[i]) for i in range(PEER_TOPK)])
                nxt = drop if nxt is None else jnp.maximum(drop, jnp.maximum(nxt, pltpu.roll(nxt, shift, 0)))
                r = _bitonic_merge_desc([jnp.maximum(r[i], other[i]) for i in range(PEER_TOPK)])
            r = r + [nxt]
            for a in range(PEER_TOPK + 1):
                packed[p][a] = r[a] if h == 0 else jnp.where(sub == h, r[a], packed[p][a])
    top = [jnp.full((8, tm), -jnp.inf, F32) for _ in range(PEER_TOPK + 1)]
    for a, b in _PEER_CANDS:
        x = packed[0][a] + packed[1][b]
        for pos in range(PEER_TOPK + 1):
            hi = jnp.maximum(top[pos], x)
            x = jnp.minimum(top[pos], x)
            top[pos] = hi
    tau = 0.5 * (top[PEER_TOPK - 1] + top[PEER_TOPK])
    z = jnp.exp(top[0] - top[0])
    for pos in range(1, PEER_TOPK):
        z = z + jnp.exp(top[pos] - top[0])
    rz = 1.0 / z
    m1, m2 = packed[0][0], packed[1][0]
    for h in range(PEER_HEADS):
        s1 = s1_scr[h]
        s2 = s2_ref[h]
        thr_ref[h] = tau[h:h + 1, :] - s1
        e1_ref[h] = jnp.exp(s1 - m1[h:h + 1, :])
        e2z_ref[h] = jnp.exp(s2 - m2[h:h + 1, :]) * rz[h:h + 1, :]


def peer_scores(xn, wq, sk, tm):
    t, d = xn.shape
    tab = jax.ShapeDtypeStruct((PEER_HEADS, PEER_NKEYS, t), F32)
    tab_spec = pl.BlockSpec((PEER_HEADS, PEER_NKEYS, tm), lambda i: (0, 0, i))
    return pl.pallas_call(
        _peer_score_kernel,
        grid=(t // tm,),
        in_specs=[pl.BlockSpec((tm, d), lambda i: (i, 0)),
                  pl.BlockSpec(wq.shape, lambda i: (0, 0)),
                  pl.BlockSpec(sk.shape, lambda i: (0, 0, 0))],
        out_specs=[tab_spec] * 4,
        out_shape=[tab] * 4,
        scratch_shapes=[pltpu.VMEM((PEER_HEADS, PEER_NKEYS, tm), F32)],
        compiler_params=pltpu.CompilerParams(dimension_semantics=("parallel",), vmem_limit_bytes=VMEM_LIMIT),
        name="peer_scores",
    )(xn, wq, sk)


def _gelu(x):
    return 0.5 * x * (1.0 + lax.erf(x * (2.0 ** -0.5)))


def _peer_main_kernel(xnT_ref, u_ref, v_ref, s2_ref, e2z_ref, thr_ref, e1_ref, o_ref, pre_scr, coef_scr):
    te, tm = pre_scr.shape
    e = pl.program_id(1)

    @pl.when(e == 0)
    def _():
        o_ref[...] = jnp.zeros_like(o_ref)

    pre_scr[...] = jnp.dot(u_ref[...], xnT_ref[...], preferred_element_type=F32)

    def body(c, carry):
        cs = pl.ds(pl.multiple_of(c * 128, 128), 128)
        for ii in range(te // PEER_NKEYS):
            rs = pl.ds(ii * PEER_NKEYS, PEER_NKEYS)
            w = jnp.zeros((PEER_NKEYS, 128), F32)
            for h in range(PEER_HEADS):
                thr = thr_ref[h, pl.ds(ii, 1), cs]
                e1 = e1_ref[h, pl.ds(ii, 1), cs]
                w = w + jnp.where(s2_ref[h, :, cs] >= thr, e2z_ref[h, :, cs], 0.0) * e1
            coef_scr[rs, cs] = (w * _gelu(pre_scr[rs, cs])).astype(BF16)
        return carry

    lax.fori_loop(0, tm // 128, body, 0)
    o_ref[...] += lax.dot_general(coef_scr[...], v_ref[...], (((0,), (0,)), ((), ())),
                                  preferred_element_type=F32)


def peer_main(xnT, ub, vb, s2, e2z, thr, e1, tm, te):
    d, t = xnT.shape
    ne = ub.shape[0]
    tab_spec = pl.BlockSpec((PEER_HEADS, PEER_NKEYS, tm), lambda i, e: (0, 0, i))
    row_spec = pl.BlockSpec((PEER_HEADS, te // PEER_NKEYS, tm), lambda i, e: (0, e, i))
    return pl.pallas_call(
        _peer_main_kernel,
        grid=(t // tm, ne // te),
        in_specs=[pl.BlockSpec((d, tm), lambda i, e: (0, i)),
                  pl.BlockSpec((te, d), lambda i, e: (e, 0)),
                  pl.BlockSpec((te, d), lambda i, e: (e, 0)),
                  tab_spec, tab_spec, row_spec, row_spec],
        out_specs=pl.BlockSpec((tm, d), lambda i, e: (i, 0)),
        out_shape=jax.ShapeDtypeStruct((t, d), F32),
        scratch_shapes=[pltpu.VMEM((te, tm), F32), pltpu.VMEM((te, tm), BF16)],
        compiler_params=pltpu.CompilerParams(dimension_semantics=("parallel", "arbitrary"),
                                             vmem_limit_bytes=56 * 1024 * 1024),
        name="peer_main",
    )(xnT, ub, vb, s2, e2z, thr, e1)


def peer_ffn(h, wq, sk, ub, vb):
    bsz, l, d = h.shape
    xn = h.reshape(bsz * l, d).astype(BF16)
    s2, e2z, thr, e1 = peer_scores(xn, wq, sk, 256)
    return peer_main(xn.T, ub, vb, s2, e2z, thr, e1, 512, 1024).reshape(bsz, l, d)


def kernel(x, c, ctx, c_ctx, ada_w, ada_b, norm1_w, norm2_w, ev_w_in, gdn_conv_w, gdn_a_log, gdn_dt_bias,
           gdn_norm_w, ssd_conv_w, ssd_conv_b, ssd_a_log, ssd_dt_bias, ssd_d, ssd_norm_w, ev_w_out,
           od_w_in, gla_w_gate, gla_b_gate, gla_norm_w, od_w_out, peer_w_q, peer_subkeys, peer_u, peer_v,
           final_norm_w):
    h, hc = x, ctx
    sc_lat = jax.nn.silu(c)
    sc_ctx = jax.nn.silu(c_ctx)
    for i in range(DEPTH):
        need_ctx = i < DEPTH - 1
        j = i // 2
        shift1, scale1, gate1, shift2, scale2, gate2 = (m[:, None, :] for m in jnp.split(sc_lat @ ada_w[i] + ada_b[i], 6, axis=-1))
        cshift1, cscale1, cgate1, cshift2, cscale2, cgate2 = jnp.split(sc_ctx @ ada_w[i] + ada_b[i], 6, axis=-1)
        a_l = rmsnorm(h, norm1_w[i]) * (1 + scale1) + shift1
        a_c = rmsnorm(hc, norm1_w[i]) * (1 + cscale1) + cshift1
        if i % 2 == 0:
            y_c, y_l = even_mixer(a_c, a_l, ev_w_in[j], gdn_conv_w[j], gdn_a_log[j], gdn_dt_bias[j], gdn_norm_w[j],
                                  ssd_conv_w[j], ssd_conv_b[j], ssd_a_log[j], ssd_dt_bias[j], ssd_d[j], ssd_norm_w[j],
                                  ev_w_out[j], need_ctx)
        else:
            y_c, y_l = odd_mixer(a_c, to_col_major(a_l), od_w_in[j], gla_w_gate[j], gla_b_gate[j], gla_norm_w[j],
                                 od_w_out[j], need_ctx)
            y_l = to_row_major(y_l)
        peer_w = (peer_w_q[i].astype(BF16),
                  peer_subkeys[i].reshape(2 * PEER_HEADS, PEER_NKEYS, PEER_DKEY // 2).astype(BF16),
                  peer_u[i].astype(BF16), peer_v[i].astype(BF16))
        h = h + gate1 * y_l
        h = h + gate2 * peer_ffn(rmsnorm(h, norm2_w[i]) * (1 + scale2) + shift2, *peer_w)
        if need_ctx:
            hc = hc + cgate1 * y_c
            hc = hc + cgate2 * peer_ffn(rmsnorm(hc, norm2_w[i]) * (1 + cscale2) + cshift2, *peer_w)
    return rmsnorm(h, final_norm_w)
```

```python
import functools
import math
import jax
import jax.numpy as jnp
from jax import lax
import numpy as np
from jax.experimental import pallas as pl
from jax.experimental.pallas import tpu as pltpu

D_MODEL = 2048
BATCH = 2
SEQ = 4096
DEPTH = 4

GRID_W = 64
CTX_LEN = 256
N_EVEN = (DEPTH + 1) // 2
N_ODD = DEPTH // 2
CHUNK = 64
CONV_K = 5
EPS = 1e-6
F32 = jnp.float32
BF16 = jnp.bfloat16

GDN_HEADS = 16
GDN_DK = 128
GDN_DV = 128
GDN_QK_DIM = GDN_HEADS * GDN_DK
GDN_V_DIM = GDN_HEADS * GDN_DV
SSD_HEADS = 32
SSD_HEADDIM = 64
SSD_INNER = SSD_HEADS * SSD_HEADDIM
SSD_GROUPS = 4
SSD_STATE = 128
SSD_XBC = SSD_INNER + 2 * SSD_GROUPS * SSD_STATE
EVEN_SPLITS = (2 * GDN_QK_DIM + GDN_V_DIM, GDN_V_DIM, 4 * GDN_HEADS, SSD_INNER, SSD_XBC, 2 * SSD_HEADS)
EVEN_IN = sum(EVEN_SPLITS)
EVEN_MIX = GDN_V_DIM + SSD_INNER
GLA_HEADS = 4
GLA_DK = D_MODEL // 2 // GLA_HEADS
GLA_DV = D_MODEL // GLA_HEADS
GLA_GATE_RANK = 16
GLA_GATE_NORMALIZER = 16.0
ODD_SPLITS = (GLA_HEADS * GLA_DK, GLA_HEADS * GLA_DK, GLA_HEADS * GLA_DV, GLA_HEADS * GLA_DV, 2 * GLA_GATE_RANK)
ODD_IN = sum(ODD_SPLITS)
ODD_MIX = GLA_HEADS * GLA_DV
PEER_HEADS = 8
PEER_NKEYS = 128
PEER_EXPERTS = PEER_NKEYS * PEER_NKEYS
PEER_DKEY = 256
PEER_TOPK = 16
PEER_BLOCK = 128

NC_CTX = CTX_LEN // CHUNK
NC_LAT = SEQ // CHUNK
ODD_IN_PAD = 6272
MXU_DT = BF16
VMEM_LIMIT = 48 * 1024 * 1024


def _mm_kernel(a_ref, b_ref, o_ref):
    o_ref[...] = jnp.dot(a_ref[...].astype(BF16), b_ref[...].astype(BF16),
                         preferred_element_type=F32).astype(o_ref.dtype)


def matmul(a, b, tm, tn, out_dtype=F32):
    m, k = a.shape
    _, n = b.shape
    assert m % tm == 0 and n % tn == 0, (m, n, tm, tn)
    return pl.pallas_call(
        _mm_kernel,
        grid=(n // tn, m // tm),
        in_specs=[pl.BlockSpec((tm, k), lambda j, i: (i, 0)),
                  pl.BlockSpec((k, tn), lambda j, i: (0, j))],
        out_specs=pl.BlockSpec((tm, tn), lambda j, i: (i, j)),
        out_shape=jax.ShapeDtypeStruct((m, n), out_dtype),
        compiler_params=pltpu.CompilerParams(dimension_semantics=("parallel", "parallel"),
                                             vmem_limit_bytes=VMEM_LIMIT),
    )(a, b)


def proj(h, w, tn):
    bsz, l, d = h.shape
    tm = 512 if (bsz * l) % 512 == 0 else 256
    return matmul(h.reshape(bsz * l, d), w, tm, tn).reshape(bsz, l, w.shape[1])


def rms_normalize(xf):
    return xf * lax.rsqrt(jnp.mean(jnp.square(xf), axis=-1, keepdims=True) + EPS)


def rmsnorm(x, w):
    return (rms_normalize(x.astype(F32)) * w.astype(F32)).astype(x.dtype)


def l2norm(x):
    xf = x.astype(F32)
    return xf * lax.rsqrt(jnp.sum(jnp.square(xf), axis=-1, keepdims=True) + EPS)


def split_cols(p, sizes):
    cuts = [int(s) for s in np.cumsum(sizes)[:-1]]
    return jnp.split(p, cuts, axis=-1)


def dwconv(x, w):
    ch = x.shape[-1]
    return lax.conv_general_dilated(x, w[:, None, :].astype(x.dtype), (1,), [(CONV_K // 2, CONV_K // 2)],
                                    dimension_numbers=('NWC', 'WIO', 'NWC'), feature_group_count=ch)


def to_col_major(h):
    b, s, d = h.shape
    rows = s // GRID_W
    return h.reshape(b, rows, GRID_W, d).transpose(0, 2, 1, 3).reshape(b, s, d)


def to_row_major(h):
    b, s, d = h.shape
    rows = s // GRID_W
    return h.reshape(b, GRID_W, rows, d).transpose(0, 2, 1, 3).reshape(b, s, d)


def _mm(a, b):
    return jnp.dot(a.astype(MXU_DT), b.astype(MXU_DT), preferred_element_type=F32)


def _mm_nt(a, b):
    return lax.dot_general(a.astype(MXU_DT), b.astype(MXU_DT), (((1,), (1,)), ((), ())), preferred_element_type=F32)


def _mm_tn(a, b):
    return lax.dot_general(a.astype(MXU_DT), b.astype(MXU_DT), (((0,), (0,)), ((), ())), preferred_element_type=F32)


def _chunk_index(n, d, nc, nl):
    bwd = jnp.where(n < nc, nc - 1 - n, 2 * nc + nl - 1 - n)
    return n + d * (bwd - n)


def _tri_masks(d):
    row = lax.broadcasted_iota(jnp.int32, (CHUNK, CHUNK), 0)
    col = lax.broadcasted_iota(jnp.int32, (CHUNK, CHUNK), 1)
    fwd = d == 0
    later = jnp.where(fwd, row, col)
    earlier = jnp.where(fwd, col, row)
    return row, col, later >= earlier, later > earlier


def _unit_tri_inverse(nmat, row, col):
    eye = (row == col).astype(F32)
    same16 = (row >> 4) == (col >> 4)
    same32 = (row >> 5) == (col >> 5)
    y = jnp.where(same16, -nmat, 0.0)
    p = eye + y
    for _ in range(3):
        y = _mm(y, y)
        p = p + _mm(p, y)
    l1 = jnp.where(jnp.logical_and(same32, jnp.logical_not(same16)), nmat, 0.0)
    p = p - _mm(_mm(p, l1), p)
    l2 = jnp.where(same32, 0.0, nmat)
    return p - _mm(_mm(p, l2), p)


def _gdn_kernel(q_ref, k_ref, v_ref, gcol_ref, grow_ref, o_ref, s_scr, *, hg, dk, dv):
    d = pl.program_id(1)
    n = pl.program_id(3)

    @pl.when(n == 0)
    def _():
        s_scr[...] = jnp.zeros_like(s_scr)

    row, col, incl, strict = _tri_masks(d)
    gcol = gcol_ref[0, 0, 0]
    grow = grow_ref[0, 0, 0, 0]
    for j in range(hg):
        q = q_ref[0, :, j * dk:(j + 1) * dk]
        k = k_ref[0, :, j * dk:(j + 1) * dk]
        v = v_ref[0, :, j * dv:(j + 1) * dv]
        beta = gcol[:, 3 * j:3 * j + 1]
        gc = gcol[:, 3 * j + 1:3 * j + 2]
        glast = gcol[:, 3 * j + 2:3 * j + 3]
        decay = jnp.where(incl, jnp.exp(jnp.where(incl, gc - grow[j:j + 1, :], 0.0)), 0.0)
        eg = jnp.exp(gc)
        kb = k * beta
        nmat = jnp.where(strict, _mm_nt(kb, k) * decay, 0.0)
        tinv = _unit_tri_inverse(nmat, row, col)
        w = _mm(tinv, kb * eg)
        u = _mm(tinv, v * beta)
        attn = _mm_nt(q, k) * decay
        s = s_scr[j]
        v_new = u - _mm(w, s)
        o_ref[0, 0, :, j * dv:(j + 1) * dv] = _mm(q * eg, s) + _mm(attn, v_new)
        s_scr[j] = s * jnp.exp(glast[0:1, :]) + _mm_tn(k * jnp.exp(glast - gc), v_new)


def gdn_scan(q, k, v, gcol, grow, nc, nl, hg):
    bsz, l, hd = q.shape
    ngrp = gcol.shape[2]
    dk = hd // (ngrp * hg)
    dv = v.shape[2] // (ngrp * hg)
    idx = lambda n, d: _chunk_index(n, d, nc, nl)
    return pl.pallas_call(
        functools.partial(_gdn_kernel, hg=hg, dk=dk, dv=dv),
        grid=(bsz, 2, ngrp, nc + nl),
        in_specs=[pl.BlockSpec((1, CHUNK, hg * dk), lambda b, d, g, n: (b, idx(n, d), g)),
                  pl.BlockSpec((1, CHUNK, hg * dk), lambda b, d, g, n: (b, idx(n, d), g)),
                  pl.BlockSpec((1, CHUNK, hg * dv), lambda b, d, g, n: (b, idx(n, d), g)),
                  pl.BlockSpec((1, 1, 1, CHUNK, 128), lambda b, d, g, n: (b, d, g, idx(n, d), 0)),
                  pl.BlockSpec((1, 1, 1, 1, 8, CHUNK), lambda b, d, g, n: (b, d, g, idx(n, d), 0, 0))],
        out_specs=pl.BlockSpec((1, 1, CHUNK, hg * dv), lambda b, d, g, n: (d, b, idx(n, d), g)),
        out_shape=jax.ShapeDtypeStruct((2, bsz, l, v.shape[2]), F32),
        scratch_shapes=[pltpu.VMEM((hg, dk, dv), F32)],
        compiler_params=pltpu.CompilerParams(
            dimension_semantics=("parallel", "parallel", "parallel", "arbitrary"), vmem_limit_bytes=VMEM_LIMIT),
        name="gdn_scan",
    )(q, k, v, gcol, grow)


def _ssd_kernel(x_ref, b_ref, c_ref, acol_ref, arow_ref, o_ref, s_scr, *, hpg, p):
    d = pl.program_id(1)
    n = pl.program_id(3)

    @pl.when(n == 0)
    def _():
        s_scr[...] = jnp.zeros_like(s_scr)

    _, _, incl, _ = _tri_masks(d)
    bm = b_ref[0]
    cm = c_ref[0]
    cb = _mm_nt(cm, bm)
    acol = acol_ref[0, 0, 0]
    arow = arow_ref[0, 0, 0, 0]
    for kh in range(hpg):
        dt = acol[:, 3 * kh:3 * kh + 1]
        acs = acol[:, 3 * kh + 1:3 * kh + 2]
        alast = acol[:, 3 * kh + 2:3 * kh + 3]
        xd = x_ref[0, :, kh * p:(kh + 1) * p] * dt
        seg = jnp.where(incl, jnp.exp(jnp.where(incl, acs - arow[kh:kh + 1, :], 0.0)), 0.0)
        s = s_scr[kh]
        o_ref[0, 0, :, kh * p:(kh + 1) * p] = _mm(cb * seg, xd) + _mm_nt(cm, s) * jnp.exp(acs)
        s_scr[kh] = s * jnp.exp(alast[0:1, :]) + _mm_tn(xd * jnp.exp(alast - acs), bm)


def ssd_scan(x, bm, cm, acol, arow, nc, nl):
    bsz, l, hp = x.shape
    ngrp = acol.shape[2]
    nstate = bm.shape[2] // ngrp
    hpg = 8
    p = hp // (ngrp * hpg)
    idx = lambda n, d: _chunk_index(n, d, nc, nl)
    return pl.pallas_call(
        functools.partial(_ssd_kernel, hpg=hpg, p=p),
        grid=(bsz, 2, ngrp, nc + nl),
        in_specs=[pl.BlockSpec((1, CHUNK, hpg * p), lambda b, d, g, n: (b, idx(n, d), g)),
                  pl.BlockSpec((1, CHUNK, nstate), lambda b, d, g, n: (b, idx(n, d), g)),
                  pl.BlockSpec((1, CHUNK, nstate), lambda b, d, g, n: (b, idx(n, d), g)),
                  pl.BlockSpec((1, 1, 1, CHUNK, 128), lambda b, d, g, n: (b, d, g, idx(n, d), 0)),
                  pl.BlockSpec((1, 1, 1, 1, 8, CHUNK), lambda b, d, g, n: (b, d, g, idx(n, d), 0, 0))],
        out_specs=pl.BlockSpec((1, 1, CHUNK, hpg * p), lambda b, d, g, n: (d, b, idx(n, d), g)),
        out_shape=jax.ShapeDtypeStruct((2, bsz, l, hp), F32),
        scratch_shapes=[pltpu.VMEM((hpg, p, nstate), F32)],
        compiler_params=pltpu.CompilerParams(
            dimension_semantics=("parallel", "parallel", "parallel", "arbitrary"), vmem_limit_bytes=VMEM_LIMIT),
        name="ssd_scan",
    )(x, bm, cm, acol, arow)


def _gla_kernel(q_ref, k_ref, v_ref, lr_ref, wg_ref, bg_ref, o_ref, st_scr, *, dk):
    d = pl.program_id(1)
    n = pl.program_id(3)

    @pl.when(n == 0)
    def _():
        st_scr[...] = jnp.zeros_like(st_scr)

    _, _, incl, _ = _tri_masks(d)
    fwd = d == 0
    lr = jnp.where(fwd, lr_ref[0, :, 0:GLA_GATE_RANK], lr_ref[0, :, GLA_GATE_RANK:2 * GLA_GATE_RANK])
    logits = _mm(lr, wg_ref[0]) + bg_ref[0]
    lg = jax.nn.log_sigmoid(logits) / GLA_GATE_NORMALIZER
    bcum = jnp.dot(incl.astype(F32), lg, preferred_element_type=F32, precision=lax.Precision.HIGHEST)
    blast = jnp.where(fwd, bcum[CHUNK - 1:CHUNK, :], bcum[0:1, :])
    q = q_ref[0] * (dk ** -0.5)
    k = k_ref[0]
    v = v_ref[0]
    q_t = q * jnp.exp(bcum)
    k_t = k * jnp.exp(-bcum)
    attn = jnp.where(incl, _mm_nt(q_t, k_t), 0.0)
    st = st_scr[...]
    o_ref[0, 0] = _mm(attn, v) + _mm_nt(q_t, st)
    st_scr[...] = st * jnp.exp(blast) + _mm_tn(v, k * jnp.exp(blast - bcum))


def gla_scan(p, wg, bg, nheads, dk, dv, nc, nl):
    bsz, l, _ = p.shape
    idx = lambda n, d: _chunk_index(n, d, nc, nl)
    kq = nheads * dk
    return pl.pallas_call(
        functools.partial(_gla_kernel, dk=dk),
        grid=(bsz, 2, nheads, nc + nl),
        in_specs=[pl.BlockSpec((1, CHUNK, dk), lambda b, d, h, n: (b, idx(n, d), h)),
                  pl.BlockSpec((1, CHUNK, dk), lambda b, d, h, n: (b, idx(n, d), nheads + h)),
                  pl.BlockSpec((1, CHUNK, dv), lambda b, d, h, n: (b, idx(n, d), 2 * kq // dv + h)),
                  pl.BlockSpec((1, CHUNK, 128), lambda b, d, h, n: (b, idx(n, d), (2 * kq + 2 * nheads * dv) // 128)),
                  pl.BlockSpec((1, GLA_GATE_RANK, dk), lambda b, d, h, n: (d, 0, h)),
                  pl.BlockSpec((1, 1, dk), lambda b, d, h, n: (d, 0, h))],
        out_specs=pl.BlockSpec((1, 1, CHUNK, dv), lambda b, d, h, n: (d, b, idx(n, d), h)),
        out_shape=jax.ShapeDtypeStruct((2, bsz, l, nheads * dv), F32),
        scratch_shapes=[pltpu.VMEM((dv, dk), F32)],
        compiler_params=pltpu.CompilerParams(
            dimension_semantics=("parallel", "parallel", "parallel", "arbitrary"), vmem_limit_bytes=VMEM_LIMIT),
        name="gla_scan",
    )(p, p, p, p, wg, bg)


def scan_gate_tables(val, logdec, nc, nl, per_group):
    bsz, l, _, h = val.shape
    nch = l // CHUNK
    ld = logdec.reshape(bsz, nch, CHUNK, 2, h)
    cf = jnp.cumsum(ld[:, :, :, 0], axis=2)
    cbk = jnp.cumsum(ld[:, :, ::-1, 1], axis=2)[:, :, ::-1]
    cum = jnp.stack([cf, cbk], axis=3)
    tot = jnp.broadcast_to(jnp.sum(ld, axis=2, keepdims=True), cum.shape)
    ngrp = h // per_group
    col = jnp.stack([val.reshape(bsz, nch, CHUNK, 2, h), cum, tot], axis=-1)
    col = col.reshape(bsz, l, 2, ngrp, per_group * 3).transpose(0, 2, 3, 1, 4)
    col = jnp.pad(col, ((0, 0),) * 4 + ((0, 128 - per_group * 3),))
    rowt = cum.reshape(bsz, nch, CHUNK, 2, ngrp, per_group).transpose(0, 3, 4, 1, 5, 2)
    rowt = jnp.pad(rowt, ((0, 0),) * 4 + ((0, 8 - per_group), (0, 0)))
    return col, rowt


def tri_masks():
    i = jnp.arange(CHUNK)
    return i[:, None] >= i[None, :], i[:, None] > i[None, :]


def chunk_heads(t):
    b, l, h, d = t.shape
    return t.reshape(b, l // CHUNK, CHUNK, h, d).transpose(0, 3, 1, 2, 4)


def chunk_scalar(t):
    b, l, h = t.shape
    return t.reshape(b, l // CHUNK, CHUNK, h).transpose(0, 3, 1, 2)


def unchunk_heads(o):
    n, b, h, c, d = o.shape
    return o.transpose(1, 0, 3, 2, 4).reshape(b, n * c, h, d)


def chunks_first(ts):
    return tuple(jnp.moveaxis(t, 2, 0) for t in ts)


def gdn_core(q, k, v, g, beta, s0):
    q, k, v = (chunk_heads(t.astype(F32)) for t in (q, k, v))
    g = jnp.cumsum(chunk_scalar(g.astype(F32)), axis=-1)
    beta = chunk_scalar(beta.astype(F32))
    incl, strict = tri_masks()
    decay = jnp.where(incl, jnp.exp(jnp.where(incl, g[..., :, None] - g[..., None, :], 0.0)), 0.0)
    kb = k * beta[..., None]
    a = jnp.where(strict, jnp.einsum('bhncd,bhnsd->bhncs', kb, k) * decay, 0.0) + jnp.eye(CHUNK, dtype=F32)
    eg = jnp.exp(g)
    w = lax.linalg.triangular_solve(a, kb * eg[..., None], left_side=True, lower=True, unit_diagonal=True)
    u = lax.linalg.triangular_solve(a, v * beta[..., None], left_side=True, lower=True, unit_diagonal=True)
    attn = jnp.einsum('bhncd,bhnsd->bhncs', q, k) * decay
    g_last = g[..., -1]
    xs = chunks_first((q * eg[..., None], w, u, attn, k * jnp.exp(g_last[..., None] - g)[..., None], jnp.exp(g_last)))

    def step(S, inp):
        q_c, w_c, u_c, a_c, k_c, dec = inp
        v_new = u_c - jnp.einsum('bhcd,bhde->bhce', w_c, S)
        o = jnp.einsum('bhcd,bhde->bhce', q_c, S) + jnp.einsum('bhcs,bhse->bhce', a_c, v_new)
        S = S * dec[..., None, None] + jnp.einsum('bhcd,bhce->bhde', k_c, v_new)
        return S, o

    S, o = lax.scan(step, s0, xs)
    return unchunk_heads(o), S


def bidirectional(core, ctx_fwd, lat_fwd, ctx_bwd, lat_bwd, s0, need_ctx):
    rev = lambda ts: tuple(t[:, ::-1] for t in ts)
    o_cf, s_f = core(*ctx_fwd, s0)
    o_lf, _ = core(*lat_fwd, s_f)
    o_cb, s_b = core(*rev(ctx_bwd), s0)
    o_lb, _ = core(*rev(lat_bwd), s_b)
    o_ctx = o_cf + o_cb[:, ::-1] if need_ctx else None
    return o_ctx, o_lf + o_lb[:, ::-1]


def conv_cat(x, w):
    return jnp.concatenate([dwconv(x[:, :CTX_LEN], w), dwconv(x[:, CTX_LEN:], w)], axis=1)


def even_mixer(hc, hl, w_in, gdn_conv_w, gdn_a_log, gdn_dt_bias, gdn_norm_w, ssd_conv_w, ssd_conv_b,
               ssd_a_log, ssd_dt_bias, ssd_d, ssd_norm_w, w_out, need_ctx):
    def prep(h):
        bsz, l, _ = h.shape
        qkv, z_a, gates, z_b, xbc, dt_raw = split_cols(proj(h, w_in, 640), EVEN_SPLITS)
        qkv = jax.nn.silu(dwconv(qkv, gdn_conv_w))
        q, k, v = split_cols(qkv, (GDN_QK_DIM, GDN_QK_DIM, GDN_V_DIM))
        q = l2norm(q.reshape(bsz, l, GDN_HEADS, GDN_DK)) * GDN_DK ** -0.5
        k = l2norm(k.reshape(bsz, l, GDN_HEADS, GDN_DK))
        v = v.reshape(bsz, l, GDN_HEADS, GDN_DV)
        gates = gates.astype(F32).reshape(bsz, l, 4, GDN_HEADS)
        beta = jax.nn.sigmoid(gates[:, :, 0:2])
        g = -jnp.exp(gdn_a_log.astype(F32)) * jax.nn.softplus(gates[:, :, 2:4] + gdn_dt_bias.astype(F32))
        gdn_f = (q, k, v, g[:, :, 0], beta[:, :, 0])
        gdn_b = (q, k, v, g[:, :, 1], beta[:, :, 1])
        xbc = jax.nn.silu(dwconv(xbc, ssd_conv_w) + ssd_conv_b)
        xs, bm, cm = split_cols(xbc, (SSD_INNER, SSD_GROUPS * SSD_STATE, SSD_GROUPS * SSD_STATE))
        dt = jax.nn.softplus(dt_raw.astype(F32).reshape(bsz, l, 2, SSD_HEADS) + ssd_dt_bias.astype(F32))
        dA = -dt * jnp.exp(ssd_a_log.astype(F32))
        return gdn_f, gdn_b, (xs, bm, cm, dt, dA), (z_a, z_b)

    def finish(o_gdn, o_ssd, z_a, z_b, xs, dtype):
        bsz, l = o_gdn.shape[:2]
        xs = xs.reshape(bsz, l, SSD_HEADS, SSD_HEADDIM)
        y_a = rms_normalize(o_gdn) * gdn_norm_w.astype(F32) * jax.nn.silu(z_a.astype(F32).reshape(bsz, l, GDN_HEADS, GDN_DV))
        y_b = (o_ssd + ssd_d.astype(F32)[:, None] * xs.astype(F32)).reshape(bsz, l, SSD_INNER) * jax.nn.silu(z_b.astype(F32))
        y_b = rms_normalize(y_b.reshape(bsz, l, SSD_GROUPS, -1)).reshape(bsz, l, SSD_INNER) * ssd_norm_w.astype(F32)
        y = jnp.concatenate([y_a.reshape(bsz, l, GDN_V_DIM), y_b], axis=-1)
        return proj(y.astype(dtype), w_out, 512)

    pc, pl_ = prep(hc), prep(hl)
    bsz = hl.shape[0]
    o_gc, o_gl = bidirectional(gdn_core, pc[0], pl_[0], pc[1], pl_[1],
                               jnp.zeros((bsz, GDN_HEADS, GDN_DK, GDN_DV), F32), need_ctx)
    xs, bm, cm, dt, dA = (jnp.concatenate([tc, tl], axis=1) for tc, tl in zip(pc[2], pl_[2]))
    acol, arow = scan_gate_tables(dt, dA, NC_CTX, NC_LAT, 8)
    o_ssd = ssd_scan(xs, bm, cm, acol, arow, NC_CTX, NC_LAT)
    o_ssd = (o_ssd[0] + o_ssd[1]).reshape(bsz, CTX_LEN + SEQ, SSD_HEADS, SSD_HEADDIM)
    y_l = finish(o_gl, o_ssd[:, CTX_LEN:], *pl_[3], pl_[2][0], hl.dtype)
    y_c = finish(o_gc, o_ssd[:, :CTX_LEN], *pc[3], pc[2][0], hc.dtype) if need_ctx else None
    return y_c, y_l


def odd_mixer(a, w_in, gla_w_gate, gla_b_gate, gla_norm_w, w_out):
    bsz, l, _ = a.shape
    p = proj(a, jnp.pad(w_in, ((0, 0), (0, ODD_IN_PAD - ODD_IN))), 896)
    o = gla_scan(p, gla_w_gate, gla_b_gate.reshape(2, 1, GLA_HEADS * GLA_DK), GLA_HEADS, GLA_DK, GLA_DV, NC_CTX, NC_LAT)
    o = (o[0] + o[1]).reshape(bsz, l, GLA_HEADS, GLA_DV)
    r = p[..., 2 * GLA_HEADS * GLA_DK + ODD_MIX:2 * GLA_HEADS * GLA_DK + 2 * ODD_MIX]
    y = rms_normalize(o) * gla_norm_w.astype(F32) * jax.nn.silu(r.astype(F32).reshape(bsz, l, GLA_HEADS, GLA_DV))
    return proj(y.reshape(bsz, l, ODD_MIX), w_out, 512)


def _ce(r, i, l):
    hi = jnp.maximum(r[i], r[l])
    lo = jnp.minimum(r[i], r[l])
    r[i], r[l] = hi, lo


def _bitonic_sort_desc(r):
    n = len(r)
    k = 2
    while k <= n:
        j = k // 2
        while j >= 1:
            for i in range(n):
                l = i ^ j
                if l > i:
                    if (i & k) == 0:
                        _ce(r, i, l)
                    else:
                        _ce(r, l, i)
            j //= 2
        k *= 2
    return r


def _bitonic_merge_desc(r):
    n = len(r)
    j = n // 2
    while j >= 1:
        for i in range(n):
            l = i ^ j
            if l > i:
                _ce(r, i, l)
        j //= 2
    return r


_PEER_CANDS = [(a, b) for a in range(PEER_TOPK + 1) for b in range(PEER_TOPK + 1) if (a + 1) * (b + 1) <= PEER_TOPK + 1]


def _peer_score_kernel(xn_ref, wq_ref, sk_ref, s2_ref, e2z_ref, thr_ref, e1_ref, s1_scr):
    tm = xn_ref.shape[0]
    nk = PEER_NKEYS
    q = jnp.dot(xn_ref[...], wq_ref[...], preferred_element_type=F32).astype(BF16)
    sub = lax.broadcasted_iota(jnp.int32, (8, tm), 0)
    packed = [[None] * (PEER_TOPK + 1), [None] * (PEER_TOPK + 1)]
    for h in range(PEER_HEADS):
        for p in range(2):
            hp = 2 * h + p
            s = lax.dot_general(sk_ref[hp], q[:, hp * (PEER_DKEY // 2):(hp + 1) * (PEER_DKEY // 2)],
                                (((1,), (1,)), ((), ())), preferred_element_type=F32)
            if p == 0:
                s1_scr[h] = s
            else:
                s2_ref[h] = s
            r = _bitonic_sort_desc([s[8 * g:8 * g + 8, :] for g in range(nk // 8)])
            nxt = None
            for shift in (4, 2, 1):
                other = [pltpu.roll(r[PEER_TOPK - 1 - i], shift, 0) for i in range(PEER_TOPK)]
                drop = functools.reduce(jnp.maximum, [jnp.minimum(r[i], other[i]) for i in range(PEER_TOPK)])
                nxt = drop if nxt is None else jnp.maximum(drop, jnp.maximum(nxt, pltpu.roll(nxt, shift, 0)))
                r = _bitonic_merge_desc([jnp.maximum(r[i], other[i]) for i in range(PEER_TOPK)])
            r = r + [nxt]
            for a in range(PEER_TOPK + 1):
                packed[p][a] = r[a] if h == 0 else jnp.where(sub == h, r[a], packed[p][a])
    top = [jnp.full((8, tm), -jnp.inf, F32) for _ in range(PEER_TOPK + 1)]
    for a, b in _PEER_CANDS:
        x = packed[0][a] + packed[1][b]
        for pos in range(PEER_TOPK + 1):
            hi = jnp.maximum(top[pos], x)
            x = jnp.minimum(top[pos], x)
            top[pos] = hi
    tau = 0.5 * (top[PEER_TOPK - 1] + top[PEER_TOPK])
    z = jnp.exp(top[0] - top[0])
    for pos in range(1, PEER_TOPK):
        z = z + jnp.exp(top[pos] - top[0])
    rz = 1.0 / z
    m1, m2 = packed[0][0], packed[1][0]
    for h in range(PEER_HEADS):
        s1 = s1_scr[h]
        s2 = s2_ref[h]
        thr_ref[h] = tau[h:h + 1, :] - s1
        e1_ref[h] = jnp.exp(s1 - m1[h:h + 1, :])
        e2z_ref[h] = jnp.exp(s2 - m2[h:h + 1, :]) * rz[h:h + 1, :]


def peer_scores(xn, wq, sk, tm):
    t, d = xn.shape
    tab = jax.ShapeDtypeStruct((PEER_HEADS, PEER_NKEYS, t), F32)
    tab_spec = pl.BlockSpec((PEER_HEADS, PEER_NKEYS, tm), lambda i: (0, 0, i))
    return pl.pallas_call(
        _peer_score_kernel,
        grid=(t // tm,),
        in_specs=[pl.BlockSpec((tm, d), lambda i: (i, 0)),
                  pl.BlockSpec(wq.shape, lambda i: (0, 0)),
                  pl.BlockSpec(sk.shape, lambda i: (0, 0, 0))],
        out_specs=[tab_spec] * 4,
        out_shape=[tab] * 4,
        scratch_shapes=[pltpu.VMEM((PEER_HEADS, PEER_NKEYS, tm), F32)],
        compiler_params=pltpu.CompilerParams(dimension_semantics=("parallel",), vmem_limit_bytes=VMEM_LIMIT),
        name="peer_scores",
    )(xn, wq, sk)


def _gelu(x):
    return 0.5 * x * (1.0 + lax.erf(x * (2.0 ** -0.5)))


def _peer_main_kernel(xnT_ref, u_ref, v_ref, s2_ref, e2z_ref, thr_ref, e1_ref, o_ref, pre_scr, coef_scr):
    te, tm = pre_scr.shape
    e = pl.program_id(1)

    @pl.when(e == 0)
    def _():
        o_ref[...] = jnp.zeros_like(o_ref)

    pre_scr[...] = jnp.dot(u_ref[...], xnT_ref[...], preferred_element_type=F32)

    def body(c, carry):
        cs = pl.ds(pl.multiple_of(c * 128, 128), 128)
        for ii in range(te // PEER_NKEYS):
            rs = pl.ds(ii * PEER_NKEYS, PEER_NKEYS)
            w = jnp.zeros((PEER_NKEYS, 128), F32)
            for h in range(PEER_HEADS):
                thr = thr_ref[h, pl.ds(ii, 1), cs]
                e1 = e1_ref[h, pl.ds(ii, 1), cs]
                w = w + jnp.where(s2_ref[h, :, cs] >= thr, e2z_ref[h, :, cs], 0.0) * e1
            coef_scr[rs, cs] = (w * _gelu(pre_scr[rs, cs])).astype(BF16)
        return carry

    lax.fori_loop(0, tm // 128, body, 0)
    o_ref[...] += lax.dot_general(coef_scr[...], v_ref[...], (((0,), (0,)), ((), ())),
                                  preferred_element_type=F32)


def peer_main(xnT, ub, vb, s2, e2z, thr, e1, tm, te):
    d, t = xnT.shape
    ne = ub.shape[0]
    tab_spec = pl.BlockSpec((PEER_HEADS, PEER_NKEYS, tm), lambda i, e: (0, 0, i))
    row_spec = pl.BlockSpec((PEER_HEADS, te // PEER_NKEYS, tm), lambda i, e: (0, e, i))
    return pl.pallas_call(
        _peer_main_kernel,
        grid=(t // tm, ne // te),
        in_specs=[pl.BlockSpec((d, tm), lambda i, e: (0, i)),
                  pl.BlockSpec((te, d), lambda i, e: (e, 0)),
                  pl.BlockSpec((te, d), lambda i, e: (e, 0)),
                  tab_spec, tab_spec, row_spec, row_spec],
        out_specs=pl.BlockSpec((tm, d), lambda i, e: (i, 0)),
        out_shape=jax.ShapeDtypeStruct((t, d), F32),
        scratch_shapes=[pltpu.VMEM((te, tm), F32), pltpu.VMEM((te, tm), BF16)],
        compiler_params=pltpu.CompilerParams(dimension_semantics=("parallel", "arbitrary"),
                                             vmem_limit_bytes=56 * 1024 * 1024),
        name="peer_main",
    )(xnT, ub, vb, s2, e2z, thr, e1)


def peer_ffn(h, wq, sk, ub, vb):
    bsz, l, d = h.shape
    xn = h.reshape(bsz * l, d).astype(BF16)
    s2, e2z, thr, e1 = peer_scores(xn, wq, sk, 256)
    return peer_main(xn.T, ub, vb, s2, e2z, thr, e1, 512, 1024).reshape(bsz, l, d)


def kernel(x, c, ctx, c_ctx, ada_w, ada_b, norm1_w, norm2_w, ev_w_in, gdn_conv_w, gdn_a_log, gdn_dt_bias,
           gdn_norm_w, ssd_conv_w, ssd_conv_b, ssd_a_log, ssd_dt_bias, ssd_d, ssd_norm_w, ev_w_out,
           od_w_in, gla_w_gate, gla_b_gate, gla_norm_w, od_w_out, peer_w_q, peer_subkeys, peer_u, peer_v,
           final_norm_w):
    h, hc = x, ctx
    sc_lat = jax.nn.silu(c)
    sc_ctx = jax.nn.silu(c_ctx)
    for i in range(DEPTH):
        need_ctx = i < DEPTH - 1
        j = i // 2
        shift1, scale1, gate1, shift2, scale2, gate2 = (m[:, None, :] for m in jnp.split(sc_lat @ ada_w[i] + ada_b[i], 6, axis=-1))
        cshift1, cscale1, cgate1, cshift2, cscale2, cgate2 = jnp.split(sc_ctx @ ada_w[i] + ada_b[i], 6, axis=-1)
        a_l = rmsnorm(h, norm1_w[i]) * (1 + scale1) + shift1
        a_c = rmsnorm(hc, norm1_w[i]) * (1 + cscale1) + cshift1
        if i % 2 == 0:
            y_c, y_l = even_mixer(a_c, a_l, ev_w_in[j], gdn_conv_w[j], gdn_a_log[j], gdn_dt_bias[j], gdn_norm_w[j],
                                  ssd_conv_w[j], ssd_conv_b[j], ssd_a_log[j], ssd_dt_bias[j], ssd_d[j], ssd_norm_w[j],
                                  ev_w_out[j], need_ctx)
        else:
            y = odd_mixer(jnp.concatenate([a_c, to_col_major(a_l)], axis=1), od_w_in[j], gla_w_gate[j], gla_b_gate[j],
                          gla_norm_w[j], od_w_out[j])
            y_c, y_l = y[:, :CTX_LEN], to_row_major(y[:, CTX_LEN:])
        peer_w = (peer_w_q[i].astype(BF16),
                  peer_subkeys[i].reshape(2 * PEER_HEADS, PEER_NKEYS, PEER_DKEY // 2).astype(BF16),
                  peer_u[i].astype(BF16), peer_v[i].astype(BF16))
        h = h + gate1 * y_l
        h = h + gate2 * peer_ffn(rmsnorm(h, norm2_w[i]) * (1 + scale2) + shift2, *peer_w)
        if need_ctx:
            hc = hc + cgate1 * y_c
            hc = hc + cgate2 * peer_ffn(rmsnorm(hc, norm2_w[i]) * (1 + cscale2) + cshift2, *peer_w)
    return rmsnorm(h, final_norm_w)
```

```python
import functools
import math
import jax
import jax.numpy as jnp
from jax import lax
import numpy as np
from jax.experimental import pallas as pl
from jax.experimental.pallas import tpu as pltpu

D_MODEL = 2048
BATCH = 2
SEQ = 4096
DEPTH = 4

GRID_W = 64
CTX_LEN = 256
N_EVEN = (DEPTH + 1) // 2
N_ODD = DEPTH // 2
CHUNK = 64
CONV_K = 5
EPS = 1e-6
F32 = jnp.float32
BF16 = jnp.bfloat16

GDN_HEADS = 16
GDN_DK = 128
GDN_DV = 128
GDN_QK_DIM = GDN_HEADS * GDN_DK
GDN_V_DIM = GDN_HEADS * GDN_DV
SSD_HEADS = 32
SSD_HEADDIM = 64
SSD_INNER = SSD_HEADS * SSD_HEADDIM
SSD_GROUPS = 4
SSD_STATE = 128
SSD_XBC = SSD_INNER + 2 * SSD_GROUPS * SSD_STATE
EVEN_SPLITS = (2 * GDN_QK_DIM + GDN_V_DIM, GDN_V_DIM, 4 * GDN_HEADS, SSD_INNER, SSD_XBC, 2 * SSD_HEADS)
EVEN_IN = sum(EVEN_SPLITS)
EVEN_MIX = GDN_V_DIM + SSD_INNER
GLA_HEADS = 4
GLA_DK = D_MODEL // 2 // GLA_HEADS
GLA_DV = D_MODEL // GLA_HEADS
GLA_GATE_RANK = 16
GLA_GATE_NORMALIZER = 16.0
ODD_SPLITS = (GLA_HEADS * GLA_DK, GLA_HEADS * GLA_DK, GLA_HEADS * GLA_DV, GLA_HEADS * GLA_DV, 2 * GLA_GATE_RANK)
ODD_IN = sum(ODD_SPLITS)
ODD_MIX = GLA_HEADS * GLA_DV
PEER_HEADS = 8
PEER_NKEYS = 128
PEER_EXPERTS = PEER_NKEYS * PEER_NKEYS
PEER_DKEY = 256
PEER_TOPK = 16
PEER_BLOCK = 128

NC_CTX = CTX_LEN // CHUNK
NC_LAT = SEQ // CHUNK
ODD_IN_PAD = 6272
MXU_DT = BF16
VMEM_LIMIT = 48 * 1024 * 1024


def _mm_kernel(a_ref, b_ref, o_ref):
    o_ref[...] = jnp.dot(a_ref[...].astype(BF16), b_ref[...].astype(BF16),
                         preferred_element_type=F32).astype(o_ref.dtype)


def matmul(a, b, tm, tn, out_dtype=F32):
    m, k = a.shape
    _, n = b.shape
    assert m % tm == 0 and n % tn == 0, (m, n, tm, tn)
    return pl.pallas_call(
        _mm_kernel,
        grid=(n // tn, m // tm),
        in_specs=[pl.BlockSpec((tm, k), lambda j, i: (i, 0)),
                  pl.BlockSpec((k, tn), lambda j, i: (0, j))],
        out_specs=pl.BlockSpec((tm, tn), lambda j, i: (i, j)),
        out_shape=jax.ShapeDtypeStruct((m, n), out_dtype),
        compiler_params=pltpu.CompilerParams(dimension_semantics=("parallel", "parallel"),
                                             vmem_limit_bytes=VMEM_LIMIT),
    )(a, b)


def proj(h, w, tn):
    bsz, l, d = h.shape
    tm = 512 if (bsz * l) % 512 == 0 else 256
    return matmul(h.reshape(bsz * l, d), w, tm, tn).reshape(bsz, l, w.shape[1])


def rms_normalize(xf):
    return xf * lax.rsqrt(jnp.mean(jnp.square(xf), axis=-1, keepdims=True) + EPS)


def rmsnorm(x, w):
    return (rms_normalize(x.astype(F32)) * w.astype(F32)).astype(x.dtype)


def l2norm(x):
    xf = x.astype(F32)
    return xf * lax.rsqrt(jnp.sum(jnp.square(xf), axis=-1, keepdims=True) + EPS)


def split_cols(p, sizes):
    cuts = [int(s) for s in np.cumsum(sizes)[:-1]]
    return jnp.split(p, cuts, axis=-1)


def dwconv(x, w):
    ch = x.shape[-1]
    return lax.conv_general_dilated(x, w[:, None, :].astype(x.dtype), (1,), [(CONV_K // 2, CONV_K // 2)],
                                    dimension_numbers=('NWC', 'WIO', 'NWC'), feature_group_count=ch)


def to_col_major(h):
    b, s, d = h.shape
    rows = s // GRID_W
    return h.reshape(b, rows, GRID_W, d).transpose(0, 2, 1, 3).reshape(b, s, d)


def to_row_major(h):
    b, s, d = h.shape
    rows = s // GRID_W
    return h.reshape(b, GRID_W, rows, d).transpose(0, 2, 1, 3).reshape(b, s, d)


def _mm(a, b):
    return jnp.dot(a.astype(MXU_DT), b.astype(MXU_DT), preferred_element_type=F32)


def _mm_nt(a, b):
    return lax.dot_general(a.astype(MXU_DT), b.astype(MXU_DT), (((1,), (1,)), ((), ())), preferred_element_type=F32)


def _mm_tn(a, b):
    return lax.dot_general(a.astype(MXU_DT), b.astype(MXU_DT), (((0,), (0,)), ((), ())), preferred_element_type=F32)


def _chunk_index(n, d, nc, nl):
    bwd = jnp.where(n < nc, nc - 1 - n, 2 * nc + nl - 1 - n)
    return n + d * (bwd - n)


def _tri_masks(d):
    row = lax.broadcasted_iota(jnp.int32, (CHUNK, CHUNK), 0)
    col = lax.broadcasted_iota(jnp.int32, (CHUNK, CHUNK), 1)
    fwd = d == 0
    later = jnp.where(fwd, row, col)
    earlier = jnp.where(fwd, col, row)
    return row, col, later >= earlier, later > earlier


def _unit_tri_inverse(nmats, row, col):
    eye = (row == col).astype(F32)
    same16 = (row >> 4) == (col >> 4)
    same32 = (row >> 5) == (col >> 5)
    mid = jnp.logical_and(same32, jnp.logical_not(same16))
    ys = [jnp.where(same16, -n, 0.0) for n in nmats]
    ps = [eye + y for y in ys]
    for _ in range(3):
        ys = [_mm(y, y) for y in ys]
        ps = [p + _mm(p, y) for p, y in zip(ps, ys)]
    ts = [_mm(p, jnp.where(mid, n, 0.0)) for p, n in zip(ps, nmats)]
    ps = [p - _mm(t, p) for p, t in zip(ps, ts)]
    ts = [_mm(p, jnp.where(same32, 0.0, n)) for p, n in zip(ps, nmats)]
    return [p - _mm(t, p) for p, t in zip(ps, ts)]


def _gdn_kernel(q_ref, k_ref, v_ref, gcol_ref, grow_ref, o_ref, s_scr, *, hg, dk, dv):
    d = pl.program_id(1)
    n = pl.program_id(3)

    @pl.when(n == 0)
    def _():
        s_scr[...] = jnp.zeros_like(s_scr)

    row, col, incl, strict = _tri_masks(d)
    gcol = gcol_ref[0, 0, 0]
    grow = grow_ref[0, 0, 0, 0]
    heads = range(hg)
    q = [q_ref[0, :, j * dk:(j + 1) * dk] for j in heads]
    k = [k_ref[0, :, j * dk:(j + 1) * dk] for j in heads]
    v = [v_ref[0, :, j * dv:(j + 1) * dv] for j in heads]
    beta = [gcol[:, 3 * j:3 * j + 1] for j in heads]
    gc = [gcol[:, 3 * j + 1:3 * j + 2] for j in heads]
    glast = [gcol[:, 3 * j + 2:3 * j + 3] for j in heads]
    decay = [jnp.where(incl, jnp.exp(jnp.where(incl, gc[j] - grow[j:j + 1, :], 0.0)), 0.0) for j in heads]
    eg = [jnp.exp(gc[j]) for j in heads]
    kb = [k[j] * beta[j] for j in heads]
    nmat = [jnp.where(strict, _mm_nt(kb[j], k[j]) * decay[j], 0.0) for j in heads]
    attn = [_mm_nt(q[j], k[j]) * decay[j] for j in heads]
    tinv = _unit_tri_inverse(nmat, row, col)
    w = [_mm(tinv[j], kb[j] * eg[j]) for j in heads]
    u = [_mm(tinv[j], v[j] * beta[j]) for j in heads]
    s = [s_scr[j] for j in heads]
    v_new = [u[j] - _mm(w[j], s[j]) for j in heads]
    qs = [_mm(q[j] * eg[j], s[j]) for j in heads]
    for j in heads:
        o_ref[0, 0, :, j * dv:(j + 1) * dv] = qs[j] + _mm(attn[j], v_new[j])
    for j in heads:
        s_scr[j] = s[j] * jnp.exp(glast[j][0:1, :]) + _mm_tn(k[j] * jnp.exp(glast[j] - gc[j]), v_new[j])


def gdn_scan(q, k, v, gcol, grow, nc, nl, hg, ndir=2):
    bsz, l, hd = q.shape
    ngrp = gcol.shape[2]
    dk = hd // (ngrp * hg)
    dv = v.shape[2] // (ngrp * hg)
    idx = lambda n, d: _chunk_index(n, d, nc, nl)
    return pl.pallas_call(
        functools.partial(_gdn_kernel, hg=hg, dk=dk, dv=dv),
        grid=(bsz, ndir, ngrp, nc + nl),
        in_specs=[pl.BlockSpec((1, CHUNK, hg * dk), lambda b, d, g, n: (b, idx(n, d), g)),
                  pl.BlockSpec((1, CHUNK, hg * dk), lambda b, d, g, n: (b, idx(n, d), g)),
                  pl.BlockSpec((1, CHUNK, hg * dv), lambda b, d, g, n: (b, idx(n, d), g)),
                  pl.BlockSpec((1, 1, 1, CHUNK, 128), lambda b, d, g, n: (b, d, g, idx(n, d), 0)),
                  pl.BlockSpec((1, 1, 1, 1, 8, CHUNK), lambda b, d, g, n: (b, d, g, idx(n, d), 0, 0))],
        out_specs=pl.BlockSpec((1, 1, CHUNK, hg * dv), lambda b, d, g, n: (d, b, idx(n, d), g)),
        out_shape=jax.ShapeDtypeStruct((ndir, bsz, l, v.shape[2]), F32),
        scratch_shapes=[pltpu.VMEM((hg, dk, dv), F32)],
        compiler_params=pltpu.CompilerParams(
            dimension_semantics=("parallel", "parallel", "parallel", "arbitrary"), vmem_limit_bytes=VMEM_LIMIT),
        name="gdn_scan",
    )(q, k, v, gcol, grow)


def _ssd_kernel(x_ref, b_ref, c_ref, acol_ref, arow_ref, o_ref, s_scr, *, hpg, p):
    d = pl.program_id(1)
    n = pl.program_id(3)

    @pl.when(n == 0)
    def _():
        s_scr[...] = jnp.zeros_like(s_scr)

    _, _, incl, _ = _tri_masks(d)
    bm = b_ref[0]
    cm = c_ref[0]
    cb = _mm_nt(cm, bm)
    acol = acol_ref[0, 0, 0]
    arow = arow_ref[0, 0, 0, 0]
    for kh in range(hpg):
        dt = acol[:, 3 * kh:3 * kh + 1]
        acs = acol[:, 3 * kh + 1:3 * kh + 2]
        alast = acol[:, 3 * kh + 2:3 * kh + 3]
        xd = x_ref[0, :, kh * p:(kh + 1) * p] * dt
        seg = jnp.where(incl, jnp.exp(jnp.where(incl, acs - arow[kh:kh + 1, :], 0.0)), 0.0)
        s = s_scr[kh]
        o_ref[0, 0, :, kh * p:(kh + 1) * p] = _mm(cb * seg, xd) + _mm_nt(cm, s) * jnp.exp(acs)
        s_scr[kh] = s * jnp.exp(alast[0:1, :]) + _mm_tn(xd * jnp.exp(alast - acs), bm)


def ssd_scan(x, bm, cm, acol, arow, nc, nl):
    bsz, l, hp = x.shape
    ngrp = acol.shape[2]
    nstate = bm.shape[2] // ngrp
    hpg = 8
    p = hp // (ngrp * hpg)
    idx = lambda n, d: _chunk_index(n, d, nc, nl)
    return pl.pallas_call(
        functools.partial(_ssd_kernel, hpg=hpg, p=p),
        grid=(bsz, 2, ngrp, nc + nl),
        in_specs=[pl.BlockSpec((1, CHUNK, hpg * p), lambda b, d, g, n: (b, idx(n, d), g)),
                  pl.BlockSpec((1, CHUNK, nstate), lambda b, d, g, n: (b, idx(n, d), g)),
                  pl.BlockSpec((1, CHUNK, nstate), lambda b, d, g, n: (b, idx(n, d), g)),
                  pl.BlockSpec((1, 1, 1, CHUNK, 128), lambda b, d, g, n: (b, d, g, idx(n, d), 0)),
                  pl.BlockSpec((1, 1, 1, 1, 8, CHUNK), lambda b, d, g, n: (b, d, g, idx(n, d), 0, 0))],
        out_specs=pl.BlockSpec((1, 1, CHUNK, hpg * p), lambda b, d, g, n: (d, b, idx(n, d), g)),
        out_shape=jax.ShapeDtypeStruct((2, bsz, l, hp), F32),
        scratch_shapes=[pltpu.VMEM((hpg, p, nstate), F32)],
        compiler_params=pltpu.CompilerParams(
            dimension_semantics=("parallel", "parallel", "parallel", "arbitrary"), vmem_limit_bytes=VMEM_LIMIT),
        name="ssd_scan",
    )(x, bm, cm, acol, arow)


def _gla_kernel(q_ref, k_ref, v_ref, lr_ref, wg_ref, bg_ref, o_ref, st_scr, *, dk):
    d = pl.program_id(1)
    n = pl.program_id(3)

    @pl.when(n == 0)
    def _():
        st_scr[...] = jnp.zeros_like(st_scr)

    _, _, incl, _ = _tri_masks(d)
    fwd = d == 0
    lr = jnp.where(fwd, lr_ref[0, :, 0:GLA_GATE_RANK], lr_ref[0, :, GLA_GATE_RANK:2 * GLA_GATE_RANK])
    logits = _mm(lr, wg_ref[0]) + bg_ref[0]
    lg = jax.nn.log_sigmoid(logits) / GLA_GATE_NORMALIZER
    bcum = jnp.dot(incl.astype(F32), lg, preferred_element_type=F32, precision=lax.Precision.HIGHEST)
    blast = jnp.where(fwd, bcum[CHUNK - 1:CHUNK, :], bcum[0:1, :])
    q = q_ref[0] * (dk ** -0.5)
    k = k_ref[0]
    v = v_ref[0]
    q_t = q * jnp.exp(bcum)
    k_t = k * jnp.exp(-bcum)
    attn = jnp.where(incl, _mm_nt(q_t, k_t), 0.0)
    st = st_scr[...]
    o_ref[0, 0] = _mm(attn, v) + _mm_nt(q_t, st)
    st_scr[...] = st * jnp.exp(blast) + _mm_tn(v, k * jnp.exp(blast - bcum))


def gla_scan(p, wg, bg, nheads, dk, dv, nc, nl):
    bsz, l, _ = p.shape
    idx = lambda n, d: _chunk_index(n, d, nc, nl)
    kq = nheads * dk
    return pl.pallas_call(
        functools.partial(_gla_kernel, dk=dk),
        grid=(bsz, 2, nheads, nc + nl),
        in_specs=[pl.BlockSpec((1, CHUNK, dk), lambda b, d, h, n: (b, idx(n, d), h)),
                  pl.BlockSpec((1, CHUNK, dk), lambda b, d, h, n: (b, idx(n, d), nheads + h)),
                  pl.BlockSpec((1, CHUNK, dv), lambda b, d, h, n: (b, idx(n, d), 2 * kq // dv + h)),
                  pl.BlockSpec((1, CHUNK, 128), lambda b, d, h, n: (b, idx(n, d), (2 * kq + 2 * nheads * dv) // 128)),
                  pl.BlockSpec((1, GLA_GATE_RANK, dk), lambda b, d, h, n: (d, 0, h)),
                  pl.BlockSpec((1, 1, dk), lambda b, d, h, n: (d, 0, h))],
        out_specs=pl.BlockSpec((1, 1, CHUNK, dv), lambda b, d, h, n: (d, b, idx(n, d), h)),
        out_shape=jax.ShapeDtypeStruct((2, bsz, l, nheads * dv), F32),
        scratch_shapes=[pltpu.VMEM((dv, dk), F32)],
        compiler_params=pltpu.CompilerParams(
            dimension_semantics=("parallel", "parallel", "parallel", "arbitrary"), vmem_limit_bytes=VMEM_LIMIT),
        name="gla_scan",
    )(p, p, p, p, wg, bg)


def scan_gate_tables(val, logdec, nc, nl, per_group, ndir=2):
    bsz, l, _, h = val.shape
    nch = l // CHUNK
    ld = logdec.reshape(bsz, nch, CHUNK, ndir, h)
    cums = [jnp.cumsum(ld[:, :, :, 0], axis=2)]
    if ndir == 2:
        cums.append(jnp.cumsum(ld[:, :, ::-1, 1], axis=2)[:, :, ::-1])
    cum = jnp.stack(cums, axis=3)
    tot = jnp.broadcast_to(jnp.sum(ld, axis=2, keepdims=True), cum.shape)
    ngrp = h // per_group
    col = jnp.stack([val.reshape(bsz, nch, CHUNK, ndir, h), cum, tot], axis=-1)
    col = col.reshape(bsz, l, ndir, ngrp, per_group * 3).transpose(0, 2, 3, 1, 4)
    col = jnp.pad(col, ((0, 0),) * 4 + ((0, 128 - per_group * 3),))
    rowt = cum.reshape(bsz, nch, CHUNK, ndir, ngrp, per_group).transpose(0, 3, 4, 1, 5, 2)
    rowt = jnp.pad(rowt, ((0, 0),) * 4 + ((0, 8 - per_group), (0, 0)))
    return col, rowt


def tri_masks():
    i = jnp.arange(CHUNK)
    return i[:, None] >= i[None, :], i[:, None] > i[None, :]


def chunk_heads(t):
    b, l, h, d = t.shape
    return t.reshape(b, l // CHUNK, CHUNK, h, d).transpose(0, 3, 1, 2, 4)


def chunk_scalar(t):
    b, l, h = t.shape
    return t.reshape(b, l // CHUNK, CHUNK, h).transpose(0, 3, 1, 2)


def unchunk_heads(o):
    n, b, h, c, d = o.shape
    return o.transpose(1, 0, 3, 2, 4).reshape(b, n * c, h, d)


def chunks_first(ts):
    return tuple(jnp.moveaxis(t, 2, 0) for t in ts)


def gdn_core(q, k, v, g, beta, s0):
    q, k, v = (chunk_heads(t.astype(F32)) for t in (q, k, v))
    g = jnp.cumsum(chunk_scalar(g.astype(F32)), axis=-1)
    beta = chunk_scalar(beta.astype(F32))
    incl, strict = tri_masks()
    decay = jnp.where(incl, jnp.exp(jnp.where(incl, g[..., :, None] - g[..., None, :], 0.0)), 0.0)
    kb = k * beta[..., None]
    a = jnp.where(strict, jnp.einsum('bhncd,bhnsd->bhncs', kb, k) * decay, 0.0) + jnp.eye(CHUNK, dtype=F32)
    eg = jnp.exp(g)
    w = lax.linalg.triangular_solve(a, kb * eg[..., None], left_side=True, lower=True, unit_diagonal=True)
    u = lax.linalg.triangular_solve(a, v * beta[..., None], left_side=True, lower=True, unit_diagonal=True)
    attn = jnp.einsum('bhncd,bhnsd->bhncs', q, k) * decay
    g_last = g[..., -1]
    xs = chunks_first((q * eg[..., None], w, u, attn, k * jnp.exp(g_last[..., None] - g)[..., None], jnp.exp(g_last)))

    def step(S, inp):
        q_c, w_c, u_c, a_c, k_c, dec = inp
        v_new = u_c - jnp.einsum('bhcd,bhde->bhce', w_c, S)
        o = jnp.einsum('bhcd,bhde->bhce', q_c, S) + jnp.einsum('bhcs,bhse->bhce', a_c, v_new)
        S = S * dec[..., None, None] + jnp.einsum('bhcd,bhce->bhde', k_c, v_new)
        return S, o

    S, o = lax.scan(step, s0, xs)
    return unchunk_heads(o), S


def bidirectional(core, ctx_fwd, lat_fwd, ctx_bwd, lat_bwd, s0, need_ctx):
    rev = lambda ts: tuple(t[:, ::-1] for t in ts)
    o_cf, s_f = core(*ctx_fwd, s0)
    o_lf, _ = core(*lat_fwd, s_f)
    o_cb, s_b = core(*rev(ctx_bwd), s0)
    o_lb, _ = core(*rev(lat_bwd), s_b)
    o_ctx = o_cf + o_cb[:, ::-1] if need_ctx else None
    return o_ctx, o_lf + o_lb[:, ::-1]


def conv_cat(x, w):
    return jnp.concatenate([dwconv(x[:, :CTX_LEN], w), dwconv(x[:, CTX_LEN:], w)], axis=1)


def even_mixer(hc, hl, w_in, gdn_conv_w, gdn_a_log, gdn_dt_bias, gdn_norm_w, ssd_conv_w, ssd_conv_b,
               ssd_a_log, ssd_dt_bias, ssd_d, ssd_norm_w, w_out, need_ctx):
    def prep(h):
        bsz, l, _ = h.shape
        qkv, z_a, gates, z_b, xbc, dt_raw = split_cols(proj(h, w_in, 640), EVEN_SPLITS)
        qkv = jax.nn.silu(dwconv(qkv, gdn_conv_w))
        q, k, v = split_cols(qkv, (GDN_QK_DIM, GDN_QK_DIM, GDN_V_DIM))
        q = l2norm(q.reshape(bsz, l, GDN_HEADS, GDN_DK)) * GDN_DK ** -0.5
        k = l2norm(k.reshape(bsz, l, GDN_HEADS, GDN_DK))
        v = v.reshape(bsz, l, GDN_HEADS, GDN_DV)
        gates = gates.astype(F32).reshape(bsz, l, 4, GDN_HEADS)
        beta = jax.nn.sigmoid(gates[:, :, 0:2])
        g = -jnp.exp(gdn_a_log.astype(F32)) * jax.nn.softplus(gates[:, :, 2:4] + gdn_dt_bias.astype(F32))
        gdn_f = (q, k, v, g[:, :, 0], beta[:, :, 0])
        gdn_b = (q, k, v, g[:, :, 1], beta[:, :, 1])
        xbc = jax.nn.silu(dwconv(xbc, ssd_conv_w) + ssd_conv_b)
        xs, bm, cm = split_cols(xbc, (SSD_INNER, SSD_GROUPS * SSD_STATE, SSD_GROUPS * SSD_STATE))
        dt = jax.nn.softplus(dt_raw.astype(F32).reshape(bsz, l, 2, SSD_HEADS) + ssd_dt_bias.astype(F32))
        dA = -dt * jnp.exp(ssd_a_log.astype(F32))
        return gdn_f, gdn_b, (xs, bm, cm, dt, dA), (z_a, z_b)

    def finish(o_gdn, o_ssd, z_a, z_b, xs, dtype):
        bsz, l = o_gdn.shape[:2]
        xs = xs.reshape(bsz, l, SSD_HEADS, SSD_HEADDIM)
        y_a = rms_normalize(o_gdn) * gdn_norm_w.astype(F32) * jax.nn.silu(z_a.astype(F32).reshape(bsz, l, GDN_HEADS, GDN_DV))
        y_b = (o_ssd + ssd_d.astype(F32)[:, None] * xs.astype(F32)).reshape(bsz, l, SSD_INNER) * jax.nn.silu(z_b.astype(F32))
        y_b = rms_normalize(y_b.reshape(bsz, l, SSD_GROUPS, -1)).reshape(bsz, l, SSD_INNER) * ssd_norm_w.astype(F32)
        y = jnp.concatenate([y_a.reshape(bsz, l, GDN_V_DIM), y_b], axis=-1)
        return proj(y.astype(dtype), w_out, 512)

    pc, pl_ = prep(hc), prep(hl)
    bsz = hl.shape[0]
    rev = lambda ts: tuple(t[:, ::-1] for t in ts)
    s0 = jnp.zeros((bsz, GDN_HEADS, GDN_DK, GDN_DV), F32)
    o_cb, s_b = gdn_core(*rev(pc[1]), s0)
    o_lb, _ = gdn_core(*rev(pl_[1]), s_b)
    cat = lambda tc, tl: jnp.concatenate([tc, tl], axis=1)
    lcat = CTX_LEN + SEQ
    gcol, grow = scan_gate_tables(cat(pc[0][4], pl_[0][4])[:, :, None, :], cat(pc[0][3], pl_[0][3])[:, :, None, :],
                                  NC_CTX, NC_LAT, 8, ndir=1)
    o_f = gdn_scan(cat(pc[0][0], pl_[0][0]).reshape(bsz, lcat, GDN_QK_DIM), cat(pc[0][1], pl_[0][1]).reshape(bsz, lcat, GDN_QK_DIM),
                   cat(pc[0][2], pl_[0][2]).reshape(bsz, lcat, GDN_V_DIM), gcol, grow, NC_CTX, NC_LAT, 8, ndir=1)[0]
    o_f = o_f.reshape(bsz, lcat, GDN_HEADS, GDN_DV)
    o_gc = o_f[:, :CTX_LEN] + o_cb[:, ::-1]
    o_gl = o_f[:, CTX_LEN:] + o_lb[:, ::-1]
    xs, bm, cm, dt, dA = (jnp.concatenate([tc, tl], axis=1) for tc, tl in zip(pc[2], pl_[2]))
    acol, arow = scan_gate_tables(dt, dA, NC_CTX, NC_LAT, 8)
    o_ssd = ssd_scan(xs, bm, cm, acol, arow, NC_CTX, NC_LAT)
    o_ssd = (o_ssd[0] + o_ssd[1]).reshape(bsz, CTX_LEN + SEQ, SSD_HEADS, SSD_HEADDIM)
    y_l = finish(o_gl, o_ssd[:, CTX_LEN:], *pl_[3], pl_[2][0], hl.dtype)
    y_c = finish(o_gc, o_ssd[:, :CTX_LEN], *pc[3], pc[2][0], hc.dtype) if need_ctx else None
    return y_c, y_l


def odd_mixer(a, w_in, gla_w_gate, gla_b_gate, gla_norm_w, w_out):
    bsz, l, _ = a.shape
    p = proj(a, jnp.pad(w_in, ((0, 0), (0, ODD_IN_PAD - ODD_IN))), 896)
    o = gla_scan(p, gla_w_gate, gla_b_gate.reshape(2, 1, GLA_HEADS * GLA_DK), GLA_HEADS, GLA_DK, GLA_DV, NC_CTX, NC_LAT)
    o = (o[0] + o[1]).reshape(bsz, l, GLA_HEADS, GLA_DV)
    r = p[..., 2 * GLA_HEADS * GLA_DK + ODD_MIX:2 * GLA_HEADS * GLA_DK + 2 * ODD_MIX]
    y = rms_normalize(o) * gla_norm_w.astype(F32) * jax.nn.silu(r.astype(F32).reshape(bsz, l, GLA_HEADS, GLA_DV))
    return proj(y.reshape(bsz, l, ODD_MIX), w_out, 512)


def _ce(r, i, l):
    hi = jnp.maximum(r[i], r[l])
    lo = jnp.minimum(r[i], r[l])
    r[i], r[l] = hi, lo


def _bitonic_sort_desc(r):
    n = len(r)
    k = 2
    while k <= n:
        j = k // 2
        while j >= 1:
            for i in range(n):
                l = i ^ j
                if l > i:
                    if (i & k) == 0:
                        _ce(r, i, l)
                    else:
                        _ce(r, l, i)
            j //= 2
        k *= 2
    return r


def _bitonic_merge_desc(r):
    n = len(r)
    j = n // 2
    while j >= 1:
        for i in range(n):
            l = i ^ j
            if l > i:
                _ce(r, i, l)
        j //= 2
    return r


_PEER_CANDS = [(a, b) for a in range(PEER_TOPK + 1) for b in range(PEER_TOPK + 1) if (a + 1) * (b + 1) <= PEER_TOPK + 1]


def _peer_score_kernel(xn_ref, wq_ref, sk_ref, s2_ref, e2z_ref, thr_ref, e1_ref, s1_scr):
    tm = xn_ref.shape[0]
    nk = PEER_NKEYS
    q = jnp.dot(xn_ref[...], wq_ref[...], preferred_element_type=F32).astype(BF16)
    sub = lax.broadcasted_iota(jnp.int32, (8, tm), 0)
    packed = [[None] * (PEER_TOPK + 1), [None] * (PEER_TOPK + 1)]
    for h in range(PEER_HEADS):
        for p in range(2):
            hp = 2 * h + p
            s = lax.dot_general(sk_ref[hp], q[:, hp * (PEER_DKEY // 2):(hp + 1) * (PEER_DKEY // 2)],
                                (((1,), (1,)), ((), ())), preferred_element_type=F32)
            if p == 0:
                s1_scr[h] = s
            else:
                s2_ref[h] = s
            r = _bitonic_sort_desc([s[8 * g:8 * g + 8, :] for g in range(nk // 8)])
            nxt = None
            for shift in (4, 2, 1):
                other = [pltpu.roll(r[PEER_TOPK - 1 - i], shift, 0) for i in range(PEER_TOPK)]
                drop = functools.reduce(jnp.maximum, [jnp.minimum(r[i], other[i]) for i in range(PEER_TOPK)])
                nxt = drop if nxt is None else jnp.maximum(drop, jnp.maximum(nxt, pltpu.roll(nxt, shift, 0)))
                r = _bitonic_merge_desc([jnp.maximum(r[i], other[i]) for i in range(PEER_TOPK)])
            r = r + [nxt]
            for a in range(PEER_TOPK + 1):
                packed[p][a] = r[a] if h == 0 else jnp.where(sub == h, r[a], packed[p][a])
    top = [jnp.full((8, tm), -jnp.inf, F32) for _ in range(PEER_TOPK + 1)]
    for a, b in _PEER_CANDS:
        x = packed[0][a] + packed[1][b]
        for pos in range(PEER_TOPK + 1):
            hi = jnp.maximum(top[pos], x)
            x = jnp.minimum(top[pos], x)
            top[pos] = hi
    tau = 0.5 * (top[PEER_TOPK - 1] + top[PEER_TOPK])
    z = jnp.exp(top[0] - top[0])
    for pos in range(1, PEER_TOPK):
        z = z + jnp.exp(top[pos] - top[0])
    rz = 1.0 / z
    m1, m2 = packed[0][0], packed[1][0]
    for h in range(PEER_HEADS):
        s1 = s1_scr[h]
        s2 = s2_ref[h]
        thr_ref[h] = tau[h:h + 1, :] - s1
        e1_ref[h] = jnp.exp(s1 - m1[h:h + 1, :])
        e2z_ref[h] = jnp.exp(s2 - m2[h:h + 1, :]) * rz[h:h + 1, :]


def peer_scores(xn, wq, sk, tm):
    t, d = xn.shape
    tab = jax.ShapeDtypeStruct((PEER_HEADS, PEER_NKEYS, t), F32)
    tab_spec = pl.BlockSpec((PEER_HEADS, PEER_NKEYS, tm), lambda i: (0, 0, i))
    return pl.pallas_call(
        _peer_score_kernel,
        grid=(t // tm,),
        in_specs=[pl.BlockSpec((tm, d), lambda i: (i, 0)),
                  pl.BlockSpec(wq.shape, lambda i: (0, 0)),
                  pl.BlockSpec(sk.shape, lambda i: (0, 0, 0))],
        out_specs=[tab_spec] * 4,
        out_shape=[tab] * 4,
        scratch_shapes=[pltpu.VMEM((PEER_HEADS, PEER_NKEYS, tm), F32)],
        compiler_params=pltpu.CompilerParams(dimension_semantics=("parallel",), vmem_limit_bytes=VMEM_LIMIT),
        name="peer_scores",
    )(xn, wq, sk)


def _gelu(x):
    return 0.5 * x * (1.0 + lax.erf(x * (2.0 ** -0.5)))


def _peer_main_kernel(xnT_ref, u_ref, v_ref, s2_ref, e2z_ref, thr_ref, e1_ref, o_ref, pre_scr, coef_scr):
    te, tm = pre_scr.shape
    e = pl.program_id(1)

    @pl.when(e == 0)
    def _():
        o_ref[...] = jnp.zeros_like(o_ref)

    pre_scr[...] = jnp.dot(u_ref[...], xnT_ref[...], preferred_element_type=F32)

    def body(c, carry):
        cs = pl.ds(pl.multiple_of(c * 128, 128), 128)
        for ii in range(te // PEER_NKEYS):
            rs = pl.ds(ii * PEER_NKEYS, PEER_NKEYS)
            w = jnp.zeros((PEER_NKEYS, 128), F32)
            for h in range(PEER_HEADS):
                thr = thr_ref[h, pl.ds(ii, 1), cs]
                e1 = e1_ref[h, pl.ds(ii, 1), cs]
                w = w + jnp.where(s2_ref[h, :, cs] >= thr, e2z_ref[h, :, cs], 0.0) * e1
            coef_scr[rs, cs] = (w * _gelu(pre_scr[rs, cs])).astype(BF16)
        return carry

    lax.fori_loop(0, tm // 128, body, 0)
    o_ref[...] += lax.dot_general(coef_scr[...], v_ref[...], (((0,), (0,)), ((), ())),
                                  preferred_element_type=F32)


def peer_main(xnT, ub, vb, s2, e2z, thr, e1, tm, te):
    d, t = xnT.shape
    ne = ub.shape[0]
    tab_spec = pl.BlockSpec((PEER_HEADS, PEER_NKEYS, tm), lambda i, e: (0, 0, i))
    row_spec = pl.BlockSpec((PEER_HEADS, te // PEER_NKEYS, tm), lambda i, e: (0, e, i))
    return pl.pallas_call(
        _peer_main_kernel,
        grid=(t // tm, ne // te),
        in_specs=[pl.BlockSpec((d, tm), lambda i, e: (0, i)),
                  pl.BlockSpec((te, d), lambda i, e: (e, 0)),
                  pl.BlockSpec((te, d), lambda i, e: (e, 0)),
                  tab_spec, tab_spec, row_spec, row_spec],
        out_specs=pl.BlockSpec((tm, d), lambda i, e: (i, 0)),
        out_shape=jax.ShapeDtypeStruct((t, d), F32),
        scratch_shapes=[pltpu.VMEM((te, tm), F32), pltpu.VMEM((te, tm), BF16)],
        compiler_params=pltpu.CompilerParams(dimension_semantics=("parallel", "arbitrary"),
                                             vmem_limit_bytes=56 * 1024 * 1024),
        name="peer_main",
    )(xnT, ub, vb, s2, e2z, thr, e1)


def peer_ffn(h, wq, sk, ub, vb):
    bsz, l, d = h.shape
    xn = h.reshape(bsz * l, d).astype(BF16)
    s2, e2z, thr, e1 = peer_scores(xn, wq, sk, 256)
    return peer_main(xn.T, ub, vb, s2, e2z, thr, e1, 512, 1024).reshape(bsz, l, d)


def kernel(x, c, ctx, c_ctx, ada_w, ada_b, norm1_w, norm2_w, ev_w_in, gdn_conv_w, gdn_a_log, gdn_dt_bias,
           gdn_norm_w, ssd_conv_w, ssd_conv_b, ssd_a_log, ssd_dt_bias, ssd_d, ssd_norm_w, ev_w_out,
           od_w_in, gla_w_gate, gla_b_gate, gla_norm_w, od_w_out, peer_w_q, peer_subkeys, peer_u, peer_v,
           final_norm_w):
    h, hc = x, ctx
    sc_lat = jax.nn.silu(c)
    sc_ctx = jax.nn.silu(c_ctx)
    for i in range(DEPTH):
        need_ctx = i < DEPTH - 1
        j = i // 2
        shift1, scale1, gate1, shift2, scale2, gate2 = (m[:, None, :] for m in jnp.split(sc_lat @ ada_w[i] + ada_b[i], 6, axis=-1))
        cshift1, cscale1, cgate1, cshift2, cscale2, cgate2 = jnp.split(sc_ctx @ ada_w[i] + ada_b[i], 6, axis=-1)
        a_l = rmsnorm(h, norm1_w[i]) * (1 + scale1) + shift1
        a_c = rmsnorm(hc, norm1_w[i]) * (1 + cscale1) + cshift1
        if i % 2 == 0:
            y_c, y_l = even_mixer(a_c, a_l, ev_w_in[j], gdn_conv_w[j], gdn_a_log[j], gdn_dt_bias[j], gdn_norm_w[j],
                                  ssd_conv_w[j], ssd_conv_b[j], ssd_a_log[j], ssd_dt_bias[j], ssd_d[j], ssd_norm_w[j],
                                  ev_w_out[j], need_ctx)
        else:
            y = odd_mixer(jnp.concatenate([a_c, to_col_major(a_l)], axis=1), od_w_in[j], gla_w_gate[j], gla_b_gate[j],
                          gla_norm_w[j], od_w_out[j])
            y_c, y_l = y[:, :CTX_LEN], to_row_major(y[:, CTX_LEN:])
        peer_w = (peer_w_q[i].astype(BF16),
                  peer_subkeys[i].reshape(2 * PEER_HEADS, PEER_NKEYS, PEER_DKEY // 2).astype(BF16),
                  peer_u[i].astype(BF16), peer_v[i].astype(BF16))
        h = h + gate1 * y_l
        h = h + gate2 * peer_ffn(rmsnorm(h, norm2_w[i]) * (1 + scale2) + shift2, *peer_w)
        if need_ctx:
            hc = hc + cgate1 * y_c
            hc = hc + cgate2 * peer_ffn(rmsnorm(hc, norm2_w[i]) * (1 + cscale2) + cshift2, *peer_w)
    return rmsnorm(h, final_norm_w)
```

```python
import functools
import math
import jax
import jax.numpy as jnp
from jax import lax
import numpy as np
from jax.experimental import pallas as pl
from jax.experimental.pallas import tpu as pltpu

D_MODEL = 2048
BATCH = 2
SEQ = 4096
DEPTH = 4

GRID_W = 64
CTX_LEN = 256
N_EVEN = (DEPTH + 1) // 2
N_ODD = DEPTH // 2
CHUNK = 64
CONV_K = 5
EPS = 1e-6
F32 = jnp.float32
BF16 = jnp.bfloat16

GDN_HEADS = 16
GDN_DK = 128
GDN_DV = 128
GDN_QK_DIM = GDN_HEADS * GDN_DK
GDN_V_DIM = GDN_HEADS * GDN_DV
SSD_HEADS = 32
SSD_HEADDIM = 64
SSD_INNER = SSD_HEADS * SSD_HEADDIM
SSD_GROUPS = 4
SSD_STATE = 128
SSD_XBC = SSD_INNER + 2 * SSD_GROUPS * SSD_STATE
EVEN_SPLITS = (2 * GDN_QK_DIM + GDN_V_DIM, GDN_V_DIM, 4 * GDN_HEADS, SSD_INNER, SSD_XBC, 2 * SSD_HEADS)
EVEN_IN = sum(EVEN_SPLITS)
EVEN_MIX = GDN_V_DIM + SSD_INNER
GLA_HEADS = 4
GLA_DK = D_MODEL // 2 // GLA_HEADS
GLA_DV = D_MODEL // GLA_HEADS
GLA_GATE_RANK = 16
GLA_GATE_NORMALIZER = 16.0
ODD_SPLITS = (GLA_HEADS * GLA_DK, GLA_HEADS * GLA_DK, GLA_HEADS * GLA_DV, GLA_HEADS * GLA_DV, 2 * GLA_GATE_RANK)
ODD_IN = sum(ODD_SPLITS)
ODD_MIX = GLA_HEADS * GLA_DV
PEER_HEADS = 8
PEER_NKEYS = 128
PEER_EXPERTS = PEER_NKEYS * PEER_NKEYS
PEER_DKEY = 256
PEER_TOPK = 16
PEER_BLOCK = 128

NC_CTX = CTX_LEN // CHUNK
NC_LAT = SEQ // CHUNK
ODD_IN_PAD = 6272
MXU_DT = BF16
VMEM_LIMIT = 48 * 1024 * 1024


def _mm_kernel(a_ref, b_ref, o_ref):
    o_ref[...] = jnp.dot(a_ref[...].astype(BF16), b_ref[...].astype(BF16),
                         preferred_element_type=F32).astype(o_ref.dtype)


def matmul(a, b, tm, tn, out_dtype=F32):
    m, k = a.shape
    _, n = b.shape
    assert m % tm == 0 and n % tn == 0, (m, n, tm, tn)
    return pl.pallas_call(
        _mm_kernel,
        grid=(n // tn, m // tm),
        in_specs=[pl.BlockSpec((tm, k), lambda j, i: (i, 0)),
                  pl.BlockSpec((k, tn), lambda j, i: (0, j))],
        out_specs=pl.BlockSpec((tm, tn), lambda j, i: (i, j)),
        out_shape=jax.ShapeDtypeStruct((m, n), out_dtype),
        compiler_params=pltpu.CompilerParams(dimension_semantics=("parallel", "parallel"),
                                             vmem_limit_bytes=VMEM_LIMIT),
    )(a, b)


def proj(h, w, tn):
    bsz, l, d = h.shape
    tm = 512 if (bsz * l) % 512 == 0 else 256
    return matmul(h.reshape(bsz * l, d), w, tm, tn).reshape(bsz, l, w.shape[1])


def rms_normalize(xf):
    return xf * lax.rsqrt(jnp.mean(jnp.square(xf), axis=-1, keepdims=True) + EPS)


def rmsnorm(x, w):
    return (rms_normalize(x.astype(F32)) * w.astype(F32)).astype(x.dtype)


def l2norm(x):
    xf = x.astype(F32)
    return xf * lax.rsqrt(jnp.sum(jnp.square(xf), axis=-1, keepdims=True) + EPS)


def split_cols(p, sizes):
    cuts = [int(s) for s in np.cumsum(sizes)[:-1]]
    return jnp.split(p, cuts, axis=-1)


def dwconv(x, w):
    ch = x.shape[-1]
    return lax.conv_general_dilated(x, w[:, None, :].astype(x.dtype), (1,), [(CONV_K // 2, CONV_K // 2)],
                                    dimension_numbers=('NWC', 'WIO', 'NWC'), feature_group_count=ch)


def to_col_major(h):
    b, s, d = h.shape
    rows = s // GRID_W
    return h.reshape(b, rows, GRID_W, d).transpose(0, 2, 1, 3).reshape(b, s, d)


def to_row_major(h):
    b, s, d = h.shape
    rows = s // GRID_W
    return h.reshape(b, GRID_W, rows, d).transpose(0, 2, 1, 3).reshape(b, s, d)


def _mm(a, b):
    return jnp.dot(a.astype(MXU_DT), b.astype(MXU_DT), preferred_element_type=F32)


def _mm_nt(a, b):
    return lax.dot_general(a.astype(MXU_DT), b.astype(MXU_DT), (((1,), (1,)), ((), ())), preferred_element_type=F32)


def _mm_tn(a, b):
    return lax.dot_general(a.astype(MXU_DT), b.astype(MXU_DT), (((0,), (0,)), ((), ())), preferred_element_type=F32)


def _chunk_index(n, d, nc, nl):
    bwd = jnp.where(n < nc, nc - 1 - n, 2 * nc + nl - 1 - n)
    return n + d * (bwd - n)


def _tri_masks(d):
    row = lax.broadcasted_iota(jnp.int32, (CHUNK, CHUNK), 0)
    col = lax.broadcasted_iota(jnp.int32, (CHUNK, CHUNK), 1)
    fwd = d == 0
    later = jnp.where(fwd, row, col)
    earlier = jnp.where(fwd, col, row)
    return row, col, later >= earlier, later > earlier


def _unit_tri_inverse(nmats, row, col):
    eye = (row == col).astype(F32)
    same16 = (row >> 4) == (col >> 4)
    same32 = (row >> 5) == (col >> 5)
    mid = jnp.logical_and(same32, jnp.logical_not(same16))
    ys = [jnp.where(same16, -n, 0.0) for n in nmats]
    ps = [eye + y for y in ys]
    for _ in range(3):
        ys = [_mm(y, y) for y in ys]
        ps = [p + _mm(p, y) for p, y in zip(ps, ys)]
    ts = [_mm(p, jnp.where(mid, n, 0.0)) for p, n in zip(ps, nmats)]
    ps = [p - _mm(t, p) for p, t in zip(ps, ts)]
    ts = [_mm(p, jnp.where(same32, 0.0, n)) for p, n in zip(ps, nmats)]
    return [p - _mm(t, p) for p, t in zip(ps, ts)]


def _gdn_kernel(q_ref, k_ref, v_ref, gcol_ref, grow_ref, o_ref, s_scr, *, hg, dk, dv):
    d = pl.program_id(1)
    n = pl.program_id(3)

    @pl.when(n == 0)
    def _():
        s_scr[...] = jnp.zeros_like(s_scr)

    row, col, incl, strict = _tri_masks(d)
    gcol = gcol_ref[0, 0, 0]
    grow = grow_ref[0, 0, 0, 0]
    heads = range(hg)
    q = [q_ref[0, :, j * dk:(j + 1) * dk] for j in heads]
    k = [k_ref[0, :, j * dk:(j + 1) * dk] for j in heads]
    v = [v_ref[0, :, j * dv:(j + 1) * dv] for j in heads]
    beta = [gcol[:, 3 * j:3 * j + 1] for j in heads]
    gc = [gcol[:, 3 * j + 1:3 * j + 2] for j in heads]
    glast = [gcol[:, 3 * j + 2:3 * j + 3] for j in heads]
    decay = [jnp.where(incl, jnp.exp(jnp.where(incl, gc[j] - grow[j:j + 1, :], 0.0)), 0.0) for j in heads]
    eg = [jnp.exp(gc[j]) for j in heads]
    kb = [k[j] * beta[j] for j in heads]
    nmat = [jnp.where(strict, _mm_nt(kb[j], k[j]) * decay[j], 0.0) for j in heads]
    attn = [_mm_nt(q[j], k[j]) * decay[j] for j in heads]
    tinv = _unit_tri_inverse(nmat, row, col)
    w = [_mm(tinv[j], kb[j] * eg[j]) for j in heads]
    u = [_mm(tinv[j], v[j] * beta[j]) for j in heads]
    s = [s_scr[j] for j in heads]
    v_new = [u[j] - _mm(w[j], s[j]) for j in heads]
    qs = [_mm(q[j] * eg[j], s[j]) for j in heads]
    for j in heads:
        o_ref[0, 0, :, j * dv:(j + 1) * dv] = qs[j] + _mm(attn[j], v_new[j])
    for j in heads:
        s_scr[j] = s[j] * jnp.exp(glast[j][0:1, :]) + _mm_tn(k[j] * jnp.exp(glast[j] - gc[j]), v_new[j])


def gdn_scan(q, k, v, gcol, grow, nc, nl, hg, ndir=2):
    bsz, l, hd = q.shape
    ngrp = gcol.shape[2]
    dk = hd // (ngrp * hg)
    dv = v.shape[2] // (ngrp * hg)
    idx = lambda n, d: _chunk_index(n, d, nc, nl)
    return pl.pallas_call(
        functools.partial(_gdn_kernel, hg=hg, dk=dk, dv=dv),
        grid=(bsz, ndir, ngrp, nc + nl),
        in_specs=[pl.BlockSpec((1, CHUNK, hg * dk), lambda b, d, g, n: (b, idx(n, d), g)),
                  pl.BlockSpec((1, CHUNK, hg * dk), lambda b, d, g, n: (b, idx(n, d), g)),
                  pl.BlockSpec((1, CHUNK, hg * dv), lambda b, d, g, n: (b, idx(n, d), g)),
                  pl.BlockSpec((1, 1, 1, CHUNK, 128), lambda b, d, g, n: (b, d, g, idx(n, d), 0)),
                  pl.BlockSpec((1, 1, 1, 1, 8, CHUNK), lambda b, d, g, n: (b, d, g, idx(n, d), 0, 0))],
        out_specs=pl.BlockSpec((1, 1, CHUNK, hg * dv), lambda b, d, g, n: (d, b, idx(n, d), g)),
        out_shape=jax.ShapeDtypeStruct((ndir, bsz, l, v.shape[2]), F32),
        scratch_shapes=[pltpu.VMEM((hg, dk, dv), F32)],
        compiler_params=pltpu.CompilerParams(
            dimension_semantics=("parallel", "parallel", "parallel", "arbitrary"), vmem_limit_bytes=VMEM_LIMIT),
        name="gdn_scan",
    )(q, k, v, gcol, grow)


def _ssd_kernel(x_ref, b_ref, c_ref, acol_ref, arow_ref, o_ref, s_scr, *, hpg, p):
    d = pl.program_id(1)
    n = pl.program_id(3)

    @pl.when(n == 0)
    def _():
        s_scr[...] = jnp.zeros_like(s_scr)

    _, _, incl, _ = _tri_masks(d)
    bm = b_ref[0]
    cm = c_ref[0]
    cb = _mm_nt(cm, bm)
    acol = acol_ref[0, 0, 0]
    arow = arow_ref[0, 0, 0, 0]
    for kh in range(hpg):
        dt = acol[:, 3 * kh:3 * kh + 1]
        acs = acol[:, 3 * kh + 1:3 * kh + 2]
        alast = acol[:, 3 * kh + 2:3 * kh + 3]
        xd = x_ref[0, :, kh * p:(kh + 1) * p] * dt
        seg = jnp.where(incl, jnp.exp(jnp.where(incl, acs - arow[kh:kh + 1, :], 0.0)), 0.0)
        s = s_scr[kh]
        o_ref[0, 0, :, kh * p:(kh + 1) * p] = _mm(cb * seg, xd) + _mm_nt(cm, s) * jnp.exp(acs)
        s_scr[kh] = s * jnp.exp(alast[0:1, :]) + _mm_tn(xd * jnp.exp(alast - acs), bm)


def ssd_scan(x, bm, cm, acol, arow, nc, nl):
    bsz, l, hp = x.shape
    ngrp = acol.shape[2]
    nstate = bm.shape[2] // ngrp
    hpg = 8
    p = hp // (ngrp * hpg)
    idx = lambda n, d: _chunk_index(n, d, nc, nl)
    return pl.pallas_call(
        functools.partial(_ssd_kernel, hpg=hpg, p=p),
        grid=(bsz, 2, ngrp, nc + nl),
        in_specs=[pl.BlockSpec((1, CHUNK, hpg * p), lambda b, d, g, n: (b, idx(n, d), g)),
                  pl.BlockSpec((1, CHUNK, nstate), lambda b, d, g, n: (b, idx(n, d), g)),
                  pl.BlockSpec((1, CHUNK, nstate), lambda b, d, g, n: (b, idx(n, d), g)),
                  pl.BlockSpec((1, 1, 1, CHUNK, 128), lambda b, d, g, n: (b, d, g, idx(n, d), 0)),
                  pl.BlockSpec((1, 1, 1, 1, 8, CHUNK), lambda b, d, g, n: (b, d, g, idx(n, d), 0, 0))],
        out_specs=pl.BlockSpec((1, 1, CHUNK, hpg * p), lambda b, d, g, n: (d, b, idx(n, d), g)),
        out_shape=jax.ShapeDtypeStruct((2, bsz, l, hp), F32),
        scratch_shapes=[pltpu.VMEM((hpg, p, nstate), F32)],
        compiler_params=pltpu.CompilerParams(
            dimension_semantics=("parallel", "parallel", "parallel", "arbitrary"), vmem_limit_bytes=VMEM_LIMIT),
        name="ssd_scan",
    )(x, bm, cm, acol, arow)


def _gla_kernel(q_ref, k_ref, v_ref, lr_ref, wg_ref, bg_ref, o_ref, st_scr, *, dk, dv, hpb):
    d = pl.program_id(1)
    n = pl.program_id(3)

    @pl.when(n == 0)
    def _():
        st_scr[...] = jnp.zeros_like(st_scr)

    _, _, incl, _ = _tri_masks(d)
    fwd = d == 0
    heads = range(hpb)
    lr = jnp.where(fwd, lr_ref[0, :, 0:GLA_GATE_RANK], lr_ref[0, :, GLA_GATE_RANK:2 * GLA_GATE_RANK])
    logits = _mm(lr, wg_ref[0]) + bg_ref[0]
    lg = jax.nn.log_sigmoid(logits) / GLA_GATE_NORMALIZER
    bcum_all = jnp.dot(incl.astype(F32), lg, preferred_element_type=F32, precision=lax.Precision.HIGHEST)
    bcum = [bcum_all[:, j * dk:(j + 1) * dk] for j in heads]
    blast = [jnp.where(fwd, bcum[j][CHUNK - 1:CHUNK, :], bcum[j][0:1, :]) for j in heads]
    q = [q_ref[0, :, j * dk:(j + 1) * dk] * (dk ** -0.5) for j in heads]
    k = [k_ref[0, :, j * dk:(j + 1) * dk] for j in heads]
    v = [v_ref[0, :, j * dv:(j + 1) * dv] for j in heads]
    q_t = [q[j] * jnp.exp(bcum[j]) for j in heads]
    k_t = [k[j] * jnp.exp(-bcum[j]) for j in heads]
    attn = [jnp.where(incl, _mm_nt(q_t[j], k_t[j]), 0.0) for j in heads]
    st = [st_scr[j] for j in heads]
    inter = [_mm_nt(q_t[j], st[j]) for j in heads]
    upd = [_mm_tn(v[j], k[j] * jnp.exp(blast[j] - bcum[j])) for j in heads]
    for j in heads:
        o_ref[0, 0, :, j * dv:(j + 1) * dv] = _mm(attn[j], v[j]) + inter[j]
    for j in heads:
        st_scr[j] = st[j] * jnp.exp(blast[j]) + upd[j]


def gla_scan(p, wg, bg, nheads, dk, dv, nc, nl, hpb=2):
    bsz, l, _ = p.shape
    idx = lambda n, d: _chunk_index(n, d, nc, nl)
    kq = nheads * dk
    ngrp = nheads // hpb
    return pl.pallas_call(
        functools.partial(_gla_kernel, dk=dk, dv=dv, hpb=hpb),
        grid=(bsz, 2, ngrp, nc + nl),
        in_specs=[pl.BlockSpec((1, CHUNK, hpb * dk), lambda b, d, h, n: (b, idx(n, d), h)),
                  pl.BlockSpec((1, CHUNK, hpb * dk), lambda b, d, h, n: (b, idx(n, d), ngrp + h)),
                  pl.BlockSpec((1, CHUNK, hpb * dv), lambda b, d, h, n: (b, idx(n, d), 2 * kq // (hpb * dv) + h)),
                  pl.BlockSpec((1, CHUNK, 128), lambda b, d, h, n: (b, idx(n, d), (2 * kq + 2 * nheads * dv) // 128)),
                  pl.BlockSpec((1, GLA_GATE_RANK, hpb * dk), lambda b, d, h, n: (d, 0, h)),
                  pl.BlockSpec((1, 1, hpb * dk), lambda b, d, h, n: (d, 0, h))],
        out_specs=pl.BlockSpec((1, 1, CHUNK, hpb * dv), lambda b, d, h, n: (d, b, idx(n, d), h)),
        out_shape=jax.ShapeDtypeStruct((2, bsz, l, nheads * dv), F32),
        scratch_shapes=[pltpu.VMEM((hpb, dv, dk), F32)],
        compiler_params=pltpu.CompilerParams(
            dimension_semantics=("parallel", "parallel", "parallel", "arbitrary"), vmem_limit_bytes=VMEM_LIMIT),
        name="gla_scan",
    )(p, p, p, p, wg, bg)


def scan_gate_tables(val, logdec, nc, nl, per_group, ndir=2):
    bsz, l, _, h = val.shape
    nch = l // CHUNK
    ld = logdec.reshape(bsz, nch, CHUNK, ndir, h)
    cums = [jnp.cumsum(ld[:, :, :, 0], axis=2)]
    if ndir == 2:
        cums.append(jnp.cumsum(ld[:, :, ::-1, 1], axis=2)[:, :, ::-1])
    cum = jnp.stack(cums, axis=3)
    tot = jnp.broadcast_to(jnp.sum(ld, axis=2, keepdims=True), cum.shape)
    ngrp = h // per_group
    col = jnp.stack([val.reshape(bsz, nch, CHUNK, ndir, h), cum, tot], axis=-1)
    col = col.reshape(bsz, l, ndir, ngrp, per_group * 3).transpose(0, 2, 3, 1, 4)
    col = jnp.pad(col, ((0, 0),) * 4 + ((0, 128 - per_group * 3),))
    rowt = cum.reshape(bsz, nch, CHUNK, ndir, ngrp, per_group).transpose(0, 3, 4, 1, 5, 2)
    rowt = jnp.pad(rowt, ((0, 0),) * 4 + ((0, 8 - per_group), (0, 0)))
    return col, rowt


def tri_masks():
    i = jnp.arange(CHUNK)
    return i[:, None] >= i[None, :], i[:, None] > i[None, :]


def chunk_heads(t):
    b, l, h, d = t.shape
    return t.reshape(b, l // CHUNK, CHUNK, h, d).transpose(0, 3, 1, 2, 4)


def chunk_scalar(t):
    b, l, h = t.shape
    return t.reshape(b, l // CHUNK, CHUNK, h).transpose(0, 3, 1, 2)


def unchunk_heads(o):
    n, b, h, c, d = o.shape
    return o.transpose(1, 0, 3, 2, 4).reshape(b, n * c, h, d)


def chunks_first(ts):
    return tuple(jnp.moveaxis(t, 2, 0) for t in ts)


def gdn_core(q, k, v, g, beta, s0):
    q, k, v = (chunk_heads(t.astype(F32)) for t in (q, k, v))
    g = jnp.cumsum(chunk_scalar(g.astype(F32)), axis=-1)
    beta = chunk_scalar(beta.astype(F32))
    incl, strict = tri_masks()
    decay = jnp.where(incl, jnp.exp(jnp.where(incl, g[..., :, None] - g[..., None, :], 0.0)), 0.0)
    kb = k * beta[..., None]
    a = jnp.where(strict, jnp.einsum('bhncd,bhnsd->bhncs', kb, k) * decay, 0.0) + jnp.eye(CHUNK, dtype=F32)
    eg = jnp.exp(g)
    w = lax.linalg.triangular_solve(a, kb * eg[..., None], left_side=True, lower=True, unit_diagonal=True)
    u = lax.linalg.triangular_solve(a, v * beta[..., None], left_side=True, lower=True, unit_diagonal=True)
    attn = jnp.einsum('bhncd,bhnsd->bhncs', q, k) * decay
    g_last = g[..., -1]
    xs = chunks_first((q * eg[..., None], w, u, attn, k * jnp.exp(g_last[..., None] - g)[..., None], jnp.exp(g_last)))

    def step(S, inp):
        q_c, w_c, u_c, a_c, k_c, dec = inp
        v_new = u_c - jnp.einsum('bhcd,bhde->bhce', w_c, S)
        o = jnp.einsum('bhcd,bhde->bhce', q_c, S) + jnp.einsum('bhcs,bhse->bhce', a_c, v_new)
        S = S * dec[..., None, None] + jnp.einsum('bhcd,bhce->bhde', k_c, v_new)
        return S, o

    S, o = lax.scan(step, s0, xs)
    return unchunk_heads(o), S


def even_mixer(hc, hl, w_in, gdn_conv_w, gdn_a_log, gdn_dt_bias, gdn_norm_w, ssd_conv_w, ssd_conv_b,
               ssd_a_log, ssd_dt_bias, ssd_d, ssd_norm_w, w_out, need_ctx):
    def prep(h):
        bsz, l, _ = h.shape
        qkv, z_a, gates, z_b, xbc, dt_raw = split_cols(proj(h, w_in, 640), EVEN_SPLITS)
        qkv = jax.nn.silu(dwconv(qkv, gdn_conv_w))
        q, k, v = split_cols(qkv, (GDN_QK_DIM, GDN_QK_DIM, GDN_V_DIM))
        q = l2norm(q.reshape(bsz, l, GDN_HEADS, GDN_DK)) * GDN_DK ** -0.5
        k = l2norm(k.reshape(bsz, l, GDN_HEADS, GDN_DK))
        v = v.reshape(bsz, l, GDN_HEADS, GDN_DV)
        gates = gates.astype(F32).reshape(bsz, l, 4, GDN_HEADS)
        beta = jax.nn.sigmoid(gates[:, :, 0:2])
        g = -jnp.exp(gdn_a_log.astype(F32)) * jax.nn.softplus(gates[:, :, 2:4] + gdn_dt_bias.astype(F32))
        gdn_f = (q, k, v, g[:, :, 0], beta[:, :, 0])
        gdn_b = (q, k, v, g[:, :, 1], beta[:, :, 1])
        xbc = jax.nn.silu(dwconv(xbc, ssd_conv_w) + ssd_conv_b)
        xs, bm, cm = split_cols(xbc, (SSD_INNER, SSD_GROUPS * SSD_STATE, SSD_GROUPS * SSD_STATE))
        dt = jax.nn.softplus(dt_raw.astype(F32).reshape(bsz, l, 2, SSD_HEADS) + ssd_dt_bias.astype(F32))
        dA = -dt * jnp.exp(ssd_a_log.astype(F32))
        return gdn_f, gdn_b, (xs, bm, cm, dt, dA), (z_a, z_b)

    def finish(o_gdn, o_ssd, z_a, z_b, xs, dtype):
        bsz, l = o_gdn.shape[:2]
        xs = xs.reshape(bsz, l, SSD_HEADS, SSD_HEADDIM)
        y_a = rms_normalize(o_gdn) * gdn_norm_w.astype(F32) * jax.nn.silu(z_a.astype(F32).reshape(bsz, l, GDN_HEADS, GDN_DV))
        y_b = (o_ssd + ssd_d.astype(F32)[:, None] * xs.astype(F32)).reshape(bsz, l, SSD_INNER) * jax.nn.silu(z_b.astype(F32))
        y_b = rms_normalize(y_b.reshape(bsz, l, SSD_GROUPS, -1)).reshape(bsz, l, SSD_INNER) * ssd_norm_w.astype(F32)
        y = jnp.concatenate([y_a.reshape(bsz, l, GDN_V_DIM), y_b], axis=-1)
        return proj(y.astype(dtype), w_out, 512)

    pc, pl_ = prep(hc), prep(hl)
    bsz = hl.shape[0]
    rev = lambda ts: tuple(t[:, ::-1] for t in ts)
    s0 = jnp.zeros((bsz, GDN_HEADS, GDN_DK, GDN_DV), F32)
    o_cb, s_b = gdn_core(*rev(pc[1]), s0)
    o_lb, _ = gdn_core(*rev(pl_[1]), s_b)
    cat = lambda tc, tl: jnp.concatenate([tc, tl], axis=1)
    lcat = CTX_LEN + SEQ
    gcol, grow = scan_gate_tables(cat(pc[0][4], pl_[0][4])[:, :, None, :], cat(pc[0][3], pl_[0][3])[:, :, None, :],
                                  NC_CTX, NC_LAT, 8, ndir=1)
    o_f = gdn_scan(cat(pc[0][0], pl_[0][0]).reshape(bsz, lcat, GDN_QK_DIM), cat(pc[0][1], pl_[0][1]).reshape(bsz, lcat, GDN_QK_DIM),
                   cat(pc[0][2], pl_[0][2]).reshape(bsz, lcat, GDN_V_DIM), gcol, grow, NC_CTX, NC_LAT, 8, ndir=1)[0]
    o_f = o_f.reshape(bsz, lcat, GDN_HEADS, GDN_DV)
    o_gc = o_f[:, :CTX_LEN] + o_cb[:, ::-1]
    o_gl = o_f[:, CTX_LEN:] + o_lb[:, ::-1]
    xs, bm, cm, dt, dA = (jnp.concatenate([tc, tl], axis=1) for tc, tl in zip(pc[2], pl_[2]))
    acol, arow = scan_gate_tables(dt, dA, NC_CTX, NC_LAT, 8)
    o_ssd = ssd_scan(xs, bm, cm, acol, arow, NC_CTX, NC_LAT)
    o_ssd = (o_ssd[0] + o_ssd[1]).reshape(bsz, CTX_LEN + SEQ, SSD_HEADS, SSD_HEADDIM)
    y_l = finish(o_gl, o_ssd[:, CTX_LEN:], *pl_[3], pl_[2][0], hl.dtype)
    y_c = finish(o_gc, o_ssd[:, :CTX_LEN], *pc[3], pc[2][0], hc.dtype) if need_ctx else None
    return y_c, y_l


def odd_mixer(a, w_in, gla_w_gate, gla_b_gate, gla_norm_w, w_out):
    bsz, l, _ = a.shape
    p = proj(a, jnp.pad(w_in, ((0, 0), (0, ODD_IN_PAD - ODD_IN))), 896)
    o = gla_scan(p, gla_w_gate, gla_b_gate.reshape(2, 1, GLA_HEADS * GLA_DK), GLA_HEADS, GLA_DK, GLA_DV, NC_CTX, NC_LAT,
                 hpb=GLA_HEADS)
    o = (o[0] + o[1]).reshape(bsz, l, GLA_HEADS, GLA_DV)
    r = p[..., 2 * GLA_HEADS * GLA_DK + ODD_MIX:2 * GLA_HEADS * GLA_DK + 2 * ODD_MIX]
    y = rms_normalize(o) * gla_norm_w.astype(F32) * jax.nn.silu(r.astype(F32).reshape(bsz, l, GLA_HEADS, GLA_DV))
    return proj(y.reshape(bsz, l, ODD_MIX), w_out, 512)


def _ce(r, i, l):
    hi = jnp.maximum(r[i], r[l])
    lo = jnp.minimum(r[i], r[l])
    r[i], r[l] = hi, lo


def _bitonic_sort_desc(r):
    n = len(r)
    k = 2
    while k <= n:
        j = k // 2
        while j >= 1:
            for i in range(n):
                l = i ^ j
                if l > i:
                    if (i & k) == 0:
                        _ce(r, i, l)
                    else:
                        _ce(r, l, i)
            j //= 2
        k *= 2
    return r


def _bitonic_merge_desc(r):
    n = len(r)
    j = n // 2
    while j >= 1:
        for i in range(n):
            l = i ^ j
            if l > i:
                _ce(r, i, l)
        j //= 2
    return r


_PEER_CANDS = [(a, b) for a in range(PEER_TOPK + 1) for b in range(PEER_TOPK + 1) if (a + 1) * (b + 1) <= PEER_TOPK + 1]


def _peer_score_kernel(xn_ref, wq_ref, sk_ref, s2_ref, e2z_ref, thr_ref, e1_ref, s1_scr):
    tm = xn_ref.shape[0]
    nk = PEER_NKEYS
    q = jnp.dot(xn_ref[...], wq_ref[...], preferred_element_type=F32).astype(BF16)
    sub = lax.broadcasted_iota(jnp.int32, (8, tm), 0)
    packed = [[None] * (PEER_TOPK + 1), [None] * (PEER_TOPK + 1)]
    for h in range(PEER_HEADS):
        for p in range(2):
            hp = 2 * h + p
            s = lax.dot_general(sk_ref[hp], q[:, hp * (PEER_DKEY // 2):(hp + 1) * (PEER_DKEY // 2)],
                                (((1,), (1,)), ((), ())), preferred_element_type=F32)
            if p == 0:
                s1_scr[h] = s
            else:
                s2_ref[h] = s
            r = _bitonic_sort_desc([s[8 * g:8 * g + 8, :] for g in range(nk // 8)])
            nxt = None
            for shift in (4, 2, 1):
                other = [pltpu.roll(r[PEER_TOPK - 1 - i], shift, 0) for i in range(PEER_TOPK)]
                drop = functools.reduce(jnp.maximum, [jnp.minimum(r[i], other[i]) for i in range(PEER_TOPK)])
                nxt = drop if nxt is None else jnp.maximum(drop, jnp.maximum(nxt, pltpu.roll(nxt, shift, 0)))
                r = _bitonic_merge_desc([jnp.maximum(r[i], other[i]) for i in range(PEER_TOPK)])
            r = r + [nxt]
            for a in range(PEER_TOPK + 1):
                packed[p][a] = r[a] if h == 0 else jnp.where(sub == h, r[a], packed[p][a])
    top = [jnp.full((8, tm), -jnp.inf, F32) for _ in range(PEER_TOPK + 1)]
    for a, b in _PEER_CANDS:
        x = packed[0][a] + packed[1][b]
        for pos in range(PEER_TOPK + 1):
            hi = jnp.maximum(top[pos], x)
            x = jnp.minimum(top[pos], x)
            top[pos] = hi
    tau = 0.5 * (top[PEER_TOPK - 1] + top[PEER_TOPK])
    z = jnp.exp(top[0] - top[0])
    for pos in range(1, PEER_TOPK):
        z = z + jnp.exp(top[pos] - top[0])
    rz = 1.0 / z
    m1, m2 = packed[0][0], packed[1][0]
    for h in range(PEER_HEADS):
        s1 = s1_scr[h]
        s2 = s2_ref[h]
        thr_ref[h] = tau[h:h + 1, :] - s1
        e1_ref[h] = jnp.exp(s1 - m1[h:h + 1, :])
        e2z_ref[h] = jnp.exp(s2 - m2[h:h + 1, :]) * rz[h:h + 1, :]


def peer_scores(xn, wq, sk, tm):
    t, d = xn.shape
    tab = jax.ShapeDtypeStruct((PEER_HEADS, PEER_NKEYS, t), F32)
    tab_spec = pl.BlockSpec((PEER_HEADS, PEER_NKEYS, tm), lambda i: (0, 0, i))
    return pl.pallas_call(
        _peer_score_kernel,
        grid=(t // tm,),
        in_specs=[pl.BlockSpec((tm, d), lambda i: (i, 0)),
                  pl.BlockSpec(wq.shape, lambda i: (0, 0)),
                  pl.BlockSpec(sk.shape, lambda i: (0, 0, 0))],
        out_specs=[tab_spec] * 4,
        out_shape=[tab] * 4,
        scratch_shapes=[pltpu.VMEM((PEER_HEADS, PEER_NKEYS, tm), F32)],
        compiler_params=pltpu.CompilerParams(dimension_semantics=("parallel",), vmem_limit_bytes=VMEM_LIMIT),
        name="peer_scores",
    )(xn, wq, sk)


def _gelu(x):
    return 0.5 * x * (1.0 + lax.erf(x * (2.0 ** -0.5)))


def _peer_main_kernel(xnT_ref, u_ref, v_ref, s2_ref, e2z_ref, thr_ref, e1_ref, o_ref, pre_scr, coef_scr):
    te, tm = pre_scr.shape
    e = pl.program_id(1)

    @pl.when(e == 0)
    def _():
        o_ref[...] = jnp.zeros_like(o_ref)

    pre_scr[...] = jnp.dot(u_ref[...], xnT_ref[...], preferred_element_type=F32)

    def body(c, carry):
        cs = pl.ds(pl.multiple_of(c * 128, 128), 128)
        for ii in range(te // PEER_NKEYS):
            rs = pl.ds(ii * PEER_NKEYS, PEER_NKEYS)
            w = jnp.zeros((PEER_NKEYS, 128), F32)
            for h in range(PEER_HEADS):
                thr = thr_ref[h, pl.ds(ii, 1), cs]
                e1 = e1_ref[h, pl.ds(ii, 1), cs]
                w = w + jnp.where(s2_ref[h, :, cs] >= thr, e2z_ref[h, :, cs], 0.0) * e1
            coef_scr[rs, cs] = (w * _gelu(pre_scr[rs, cs])).astype(BF16)
        return carry

    lax.fori_loop(0, tm // 128, body, 0)
    o_ref[...] += lax.dot_general(coef_scr[...], v_ref[...], (((0,), (0,)), ((), ())),
                                  preferred_element_type=F32)


def peer_main(xnT, ub, vb, s2, e2z, thr, e1, tm, te):
    d, t = xnT.shape
    ne = ub.shape[0]
    tab_spec = pl.BlockSpec((PEER_HEADS, PEER_NKEYS, tm), lambda i, e: (0, 0, i))
    row_spec = pl.BlockSpec((PEER_HEADS, te // PEER_NKEYS, tm), lambda i, e: (0, e, i))
    return pl.pallas_call(
        _peer_main_kernel,
        grid=(t // tm, ne // te),
        in_specs=[pl.BlockSpec((d, tm), lambda i, e: (0, i)),
                  pl.BlockSpec((te, d), lambda i, e: (e, 0)),
                  pl.BlockSpec((te, d), lambda i, e: (e, 0)),
                  tab_spec, tab_spec, row_spec, row_spec],
        out_specs=pl.BlockSpec((tm, d), lambda i, e: (i, 0)),
        out_shape=jax.ShapeDtypeStruct((t, d), F32),
        scratch_shapes=[pltpu.VMEM((te, tm), F32), pltpu.VMEM((te, tm), BF16)],
        compiler_params=pltpu.CompilerParams(dimension_semantics=("parallel", "arbitrary"),
                                             vmem_limit_bytes=56 * 1024 * 1024),
        name="peer_main",
    )(xnT, ub, vb, s2, e2z, thr, e1)


def peer_ffn(h, wq, sk, ub, vb):
    bsz, l, d = h.shape
    xn = h.reshape(bsz * l, d).astype(BF16)
    s2, e2z, thr, e1 = peer_scores(xn, wq, sk, 256)
    return peer_main(xn.T, ub, vb, s2, e2z, thr, e1, 512, 1024).reshape(bsz, l, d)


def kernel(x, c, ctx, c_ctx, ada_w, ada_b, norm1_w, norm2_w, ev_w_in, gdn_conv_w, gdn_a_log, gdn_dt_bias,
           gdn_norm_w, ssd_conv_w, ssd_conv_b, ssd_a_log, ssd_dt_bias, ssd_d, ssd_norm_w, ev_w_out,
           od_w_in, gla_w_gate, gla_b_gate, gla_norm_w, od_w_out, peer_w_q, peer_subkeys, peer_u, peer_v,
           final_norm_w):
    h, hc = x, ctx
    sc_lat = jax.nn.silu(c)
    sc_ctx = jax.nn.silu(c_ctx)
    for i in range(DEPTH):
        need_ctx = i < DEPTH - 1
        j = i // 2
        shift1, scale1, gate1, shift2, scale2, gate2 = (m[:, None, :] for m in jnp.split(sc_lat @ ada_w[i] + ada_b[i], 6, axis=-1))
        cshift1, cscale1, cgate1, cshift2, cscale2, cgate2 = jnp.split(sc_ctx @ ada_w[i] + ada_b[i], 6, axis=-1)
        a_l = rmsnorm(h, norm1_w[i]) * (1 + scale1) + shift1
        a_c = rmsnorm(hc, norm1_w[i]) * (1 + cscale1) + cshift1
        if i % 2 == 0:
            y_c, y_l = even_mixer(a_c, a_l, ev_w_in[j], gdn_conv_w[j], gdn_a_log[j], gdn_dt_bias[j], gdn_norm_w[j],
                                  ssd_conv_w[j], ssd_conv_b[j], ssd_a_log[j], ssd_dt_bias[j], ssd_d[j], ssd_norm_w[j],
                                  ev_w_out[j], need_ctx)
        else:
            y = odd_mixer(jnp.concatenate([a_c, to_col_major(a_l)], axis=1), od_w_in[j], gla_w_gate[j], gla_b_gate[j],
                          gla_norm_w[j], od_w_out[j])
            y_c, y_l = y[:, :CTX_LEN], to_row_major(y[:, CTX_LEN:])
        peer_w = (peer_w_q[i].astype(BF16),
                  peer_subkeys[i].reshape(2 * PEER_HEADS, PEER_NKEYS, PEER_DKEY // 2).astype(BF16),
                  peer_u[i].astype(BF16), peer_v[i].astype(BF16))
        h = h + gate1 * y_l
        h = h + gate2 * peer_ffn(rmsnorm(h, norm2_w[i]) * (1 + scale2) + shift2, *peer_w)
        if need_ctx:
            hc = hc + cgate1 * y_c
            hc = hc + cgate2 * peer_ffn(rmsnorm(hc, norm2_w[i]) * (1 + cscale2) + cshift2, *peer_w)
    return rmsnorm(h, final_norm_w)
```

```python
import functools
import math
import jax
import jax.numpy as jnp
from jax import lax
import numpy as np
from jax.experimental import pallas as pl
from jax.experimental.pallas import tpu as pltpu

D_MODEL = 2048
BATCH = 2
SEQ = 4096
DEPTH = 4

GRID_W = 64
CTX_LEN = 256
N_EVEN = (DEPTH + 1) // 2
N_ODD = DEPTH // 2
CHUNK = 64
CONV_K = 5
EPS = 1e-6
F32 = jnp.float32
BF16 = jnp.bfloat16

GDN_HEADS = 16
GDN_DK = 128
GDN_DV = 128
GDN_QK_DIM = GDN_HEADS * GDN_DK
GDN_V_DIM = GDN_HEADS * GDN_DV
SSD_HEADS = 32
SSD_HEADDIM = 64
SSD_INNER = SSD_HEADS * SSD_HEADDIM
SSD_GROUPS = 4
SSD_STATE = 128
SSD_XBC = SSD_INNER + 2 * SSD_GROUPS * SSD_STATE
EVEN_SPLITS = (2 * GDN_QK_DIM + GDN_V_DIM, GDN_V_DIM, 4 * GDN_HEADS, SSD_INNER, SSD_XBC, 2 * SSD_HEADS)
EVEN_IN = sum(EVEN_SPLITS)
EVEN_MIX = GDN_V_DIM + SSD_INNER
GLA_HEADS = 4
GLA_DK = D_MODEL // 2 // GLA_HEADS
GLA_DV = D_MODEL // GLA_HEADS
GLA_GATE_RANK = 16
GLA_GATE_NORMALIZER = 16.0
ODD_SPLITS = (GLA_HEADS * GLA_DK, GLA_HEADS * GLA_DK, GLA_HEADS * GLA_DV, GLA_HEADS * GLA_DV, 2 * GLA_GATE_RANK)
ODD_IN = sum(ODD_SPLITS)
ODD_MIX = GLA_HEADS * GLA_DV
PEER_HEADS = 8
PEER_NKEYS = 128
PEER_EXPERTS = PEER_NKEYS * PEER_NKEYS
PEER_DKEY = 256
PEER_TOPK = 16
PEER_BLOCK = 128

NC_CTX = CTX_LEN // CHUNK
NC_LAT = SEQ // CHUNK
ODD_IN_PAD = 6272
MXU_DT = BF16
VMEM_LIMIT = 48 * 1024 * 1024


def _mm_kernel(a_ref, b_ref, o_ref):
    o_ref[...] = jnp.dot(a_ref[...].astype(BF16), b_ref[...].astype(BF16),
                         preferred_element_type=F32).astype(o_ref.dtype)


def matmul(a, b, tm, tn, out_dtype=F32):
    m, k = a.shape
    _, n = b.shape
    assert m % tm == 0 and n % tn == 0, (m, n, tm, tn)
    return pl.pallas_call(
        _mm_kernel,
        grid=(n // tn, m // tm),
        in_specs=[pl.BlockSpec((tm, k), lambda j, i: (i, 0)),
                  pl.BlockSpec((k, tn), lambda j, i: (0, j))],
        out_specs=pl.BlockSpec((tm, tn), lambda j, i: (i, j)),
        out_shape=jax.ShapeDtypeStruct((m, n), out_dtype),
        compiler_params=pltpu.CompilerParams(dimension_semantics=("parallel", "parallel"),
                                             vmem_limit_bytes=VMEM_LIMIT),
    )(a, b)


def proj(h, w, tn):
    bsz, l, d = h.shape
    tm = 512 if (bsz * l) % 512 == 0 else 256
    return matmul(h.reshape(bsz * l, d), w, tm, tn).reshape(bsz, l, w.shape[1])


def rms_normalize(xf):
    return xf * lax.rsqrt(jnp.mean(jnp.square(xf), axis=-1, keepdims=True) + EPS)


def rmsnorm(x, w):
    return (rms_normalize(x.astype(F32)) * w.astype(F32)).astype(x.dtype)


def l2norm(x):
    xf = x.astype(F32)
    return xf * lax.rsqrt(jnp.sum(jnp.square(xf), axis=-1, keepdims=True) + EPS)


def split_cols(p, sizes):
    cuts = [int(s) for s in np.cumsum(sizes)[:-1]]
    return jnp.split(p, cuts, axis=-1)


def dwconv(x, w):
    ch = x.shape[-1]
    return lax.conv_general_dilated(x, w[:, None, :].astype(x.dtype), (1,), [(CONV_K // 2, CONV_K // 2)],
                                    dimension_numbers=('NWC', 'WIO', 'NWC'), feature_group_count=ch)


def to_col_major(h):
    b, s, d = h.shape
    rows = s // GRID_W
    return h.reshape(b, rows, GRID_W, d).transpose(0, 2, 1, 3).reshape(b, s, d)


def to_row_major(h):
    b, s, d = h.shape
    rows = s // GRID_W
    return h.reshape(b, GRID_W, rows, d).transpose(0, 2, 1, 3).reshape(b, s, d)


def _mm(a, b):
    return jnp.dot(a.astype(MXU_DT), b.astype(MXU_DT), preferred_element_type=F32)


def _mm_nt(a, b):
    return lax.dot_general(a.astype(MXU_DT), b.astype(MXU_DT), (((1,), (1,)), ((), ())), preferred_element_type=F32)


def _mm_tn(a, b):
    return lax.dot_general(a.astype(MXU_DT), b.astype(MXU_DT), (((0,), (0,)), ((), ())), preferred_element_type=F32)


def _chunk_index(n, d, nc, nl):
    bwd = jnp.where(n < nc, nc - 1 - n, 2 * nc + nl - 1 - n)
    return n + d * (bwd - n)


def _tri_masks(d):
    row = lax.broadcasted_iota(jnp.int32, (CHUNK, CHUNK), 0)
    col = lax.broadcasted_iota(jnp.int32, (CHUNK, CHUNK), 1)
    fwd = d == 0
    later = jnp.where(fwd, row, col)
    earlier = jnp.where(fwd, col, row)
    return row, col, later >= earlier, later > earlier


def _unit_tri_inverse(nmats, row, col):
    eye = (row == col).astype(F32)
    same16 = (row >> 4) == (col >> 4)
    same32 = (row >> 5) == (col >> 5)
    mid = jnp.logical_and(same32, jnp.logical_not(same16))
    ys = [jnp.where(same16, -n, 0.0) for n in nmats]
    ps = [eye + y for y in ys]
    for _ in range(3):
        ys = [_mm(y, y) for y in ys]
        ps = [p + _mm(p, y) for p, y in zip(ps, ys)]
    ts = [_mm(p, jnp.where(mid, n, 0.0)) for p, n in zip(ps, nmats)]
    ps = [p - _mm(t, p) for p, t in zip(ps, ts)]
    ts = [_mm(p, jnp.where(same32, 0.0, n)) for p, n in zip(ps, nmats)]
    return [p - _mm(t, p) for p, t in zip(ps, ts)]


def _gdn_kernel(q_ref, k_ref, v_ref, gcol_ref, grow_ref, o_ref, s_scr, *, hg, dk, dv):
    d = pl.program_id(1)
    n = pl.program_id(3)

    @pl.when(n == 0)
    def _():
        s_scr[...] = jnp.zeros_like(s_scr)

    row, col, incl, strict = _tri_masks(d)
    gcol = gcol_ref[0, 0, 0]
    grow = grow_ref[0, 0, 0, 0]
    heads = range(hg)
    q = [q_ref[0, :, j * dk:(j + 1) * dk] for j in heads]
    k = [k_ref[0, :, j * dk:(j + 1) * dk] for j in heads]
    v = [v_ref[0, :, j * dv:(j + 1) * dv] for j in heads]
    beta = [gcol[:, 3 * j:3 * j + 1] for j in heads]
    gc = [gcol[:, 3 * j + 1:3 * j + 2] for j in heads]
    glast = [gcol[:, 3 * j + 2:3 * j + 3] for j in heads]
    decay = [jnp.where(incl, jnp.exp(jnp.where(incl, gc[j] - grow[j:j + 1, :], 0.0)), 0.0) for j in heads]
    eg = [jnp.exp(gc[j]) for j in heads]
    kb = [k[j] * beta[j] for j in heads]
    nmat = [jnp.where(strict, _mm_nt(kb[j], k[j]) * decay[j], 0.0) for j in heads]
    attn = [_mm_nt(q[j], k[j]) * decay[j] for j in heads]
    tinv = _unit_tri_inverse(nmat, row, col)
    w = [_mm(tinv[j], kb[j] * eg[j]) for j in heads]
    u = [_mm(tinv[j], v[j] * beta[j]) for j in heads]
    s = [s_scr[j] for j in heads]
    v_new = [u[j] - _mm(w[j], s[j]) for j in heads]
    qs = [_mm(q[j] * eg[j], s[j]) for j in heads]
    for j in heads:
        o_ref[0, 0, :, j * dv:(j + 1) * dv] = qs[j] + _mm(attn[j], v_new[j])
    for j in heads:
        s_scr[j] = s[j] * jnp.exp(glast[j][0:1, :]) + _mm_tn(k[j] * jnp.exp(glast[j] - gc[j]), v_new[j])


def gdn_scan(q, k, v, gcol, grow, nc, nl, hg, ndir=2):
    bsz, l, hd = q.shape
    ngrp = gcol.shape[2]
    dk = hd // (ngrp * hg)
    dv = v.shape[2] // (ngrp * hg)
    idx = lambda n, d: _chunk_index(n, d, nc, nl)
    return pl.pallas_call(
        functools.partial(_gdn_kernel, hg=hg, dk=dk, dv=dv),
        grid=(bsz, ndir, ngrp, nc + nl),
        in_specs=[pl.BlockSpec((1, CHUNK, hg * dk), lambda b, d, g, n: (b, idx(n, d), g)),
                  pl.BlockSpec((1, CHUNK, hg * dk), lambda b, d, g, n: (b, idx(n, d), g)),
                  pl.BlockSpec((1, CHUNK, hg * dv), lambda b, d, g, n: (b, idx(n, d), g)),
                  pl.BlockSpec((1, 1, 1, CHUNK, 128), lambda b, d, g, n: (b, d, g, idx(n, d), 0)),
                  pl.BlockSpec((1, 1, 1, 1, 8, CHUNK), lambda b, d, g, n: (b, d, g, idx(n, d), 0, 0))],
        out_specs=pl.BlockSpec((1, 1, CHUNK, hg * dv), lambda b, d, g, n: (d, b, idx(n, d), g)),
        out_shape=jax.ShapeDtypeStruct((ndir, bsz, l, v.shape[2]), F32),
        scratch_shapes=[pltpu.VMEM((hg, dk, dv), F32)],
        compiler_params=pltpu.CompilerParams(
            dimension_semantics=("parallel", "parallel", "parallel", "arbitrary"), vmem_limit_bytes=VMEM_LIMIT),
        name="gdn_scan",
    )(q, k, v, gcol, grow)


def _ssd_kernel(x_ref, b_ref, c_ref, acol_ref, arow_ref, o_ref, s_scr, *, hpg, p):
    d = pl.program_id(1)
    n = pl.program_id(3)

    @pl.when(n == 0)
    def _():
        s_scr[...] = jnp.zeros_like(s_scr)

    _, _, incl, _ = _tri_masks(d)
    bm = b_ref[0]
    cm = c_ref[0]
    cb = _mm_nt(cm, bm)
    acol = acol_ref[0, 0, 0]
    arow = arow_ref[0, 0, 0, 0]
    for kh in range(hpg):
        dt = acol[:, 3 * kh:3 * kh + 1]
        acs = acol[:, 3 * kh + 1:3 * kh + 2]
        alast = acol[:, 3 * kh + 2:3 * kh + 3]
        xd = x_ref[0, :, kh * p:(kh + 1) * p] * dt
        seg = jnp.where(incl, jnp.exp(jnp.where(incl, acs - arow[kh:kh + 1, :], 0.0)), 0.0)
        s = s_scr[kh]
        o_ref[0, 0, :, kh * p:(kh + 1) * p] = _mm(cb * seg, xd) + _mm_nt(cm, s) * jnp.exp(acs)
        s_scr[kh] = s * jnp.exp(alast[0:1, :]) + _mm_tn(xd * jnp.exp(alast - acs), bm)


def ssd_scan(x, bm, cm, acol, arow, nc, nl):
    bsz, l, hp = x.shape
    ngrp = acol.shape[2]
    nstate = bm.shape[2] // ngrp
    hpg = 8
    p = hp // (ngrp * hpg)
    idx = lambda n, d: _chunk_index(n, d, nc, nl)
    return pl.pallas_call(
        functools.partial(_ssd_kernel, hpg=hpg, p=p),
        grid=(bsz, 2, ngrp, nc + nl),
        in_specs=[pl.BlockSpec((1, CHUNK, hpg * p), lambda b, d, g, n: (b, idx(n, d), g)),
                  pl.BlockSpec((1, CHUNK, nstate), lambda b, d, g, n: (b, idx(n, d), g)),
                  pl.BlockSpec((1, CHUNK, nstate), lambda b, d, g, n: (b, idx(n, d), g)),
                  pl.BlockSpec((1, 1, 1, CHUNK, 128), lambda b, d, g, n: (b, d, g, idx(n, d), 0)),
                  pl.BlockSpec((1, 1, 1, 1, 8, CHUNK), lambda b, d, g, n: (b, d, g, idx(n, d), 0, 0))],
        out_specs=pl.BlockSpec((1, 1, CHUNK, hpg * p), lambda b, d, g, n: (d, b, idx(n, d), g)),
        out_shape=jax.ShapeDtypeStruct((2, bsz, l, hp), F32),
        scratch_shapes=[pltpu.VMEM((hpg, p, nstate), F32)],
        compiler_params=pltpu.CompilerParams(
            dimension_semantics=("parallel", "parallel", "parallel", "arbitrary"), vmem_limit_bytes=VMEM_LIMIT),
        name="ssd_scan",
    )(x, bm, cm, acol, arow)


def _gla_kernel(q_ref, k_ref, v_ref, lr_ref, wg_ref, bg_ref, o_ref, st_scr, *, dk, dv, hpb):
    d = pl.program_id(1)
    n = pl.program_id(3)

    @pl.when(n == 0)
    def _():
        st_scr[...] = jnp.zeros_like(st_scr)

    _, _, incl, _ = _tri_masks(d)
    fwd = d == 0
    heads = range(hpb)
    lr = jnp.where(fwd, lr_ref[0, :, 0:GLA_GATE_RANK], lr_ref[0, :, GLA_GATE_RANK:2 * GLA_GATE_RANK])
    logits = _mm(lr, wg_ref[0]) + bg_ref[0]
    lg = jax.nn.log_sigmoid(logits) / GLA_GATE_NORMALIZER
    bcum_all = jnp.dot(incl.astype(F32), lg, preferred_element_type=F32, precision=lax.Precision.HIGHEST)
    bcum = [bcum_all[:, j * dk:(j + 1) * dk] for j in heads]
    blast = [jnp.where(fwd, bcum[j][CHUNK - 1:CHUNK, :], bcum[j][0:1, :]) for j in heads]
    q = [q_ref[0, :, j * dk:(j + 1) * dk] * (dk ** -0.5) for j in heads]
    k = [k_ref[0, :, j * dk:(j + 1) * dk] for j in heads]
    v = [v_ref[0, :, j * dv:(j + 1) * dv] for j in heads]
    q_t = [q[j] * jnp.exp(bcum[j]) for j in heads]
    k_t = [k[j] * jnp.exp(-bcum[j]) for j in heads]
    attn = [jnp.where(incl, _mm_nt(q_t[j], k_t[j]), 0.0) for j in heads]
    st = [st_scr[j] for j in heads]
    inter = [_mm_nt(q_t[j], st[j]) for j in heads]
    upd = [_mm_tn(v[j], k[j] * jnp.exp(blast[j] - bcum[j])) for j in heads]
    for j in heads:
        o_ref[0, 0, :, j * dv:(j + 1) * dv] = _mm(attn[j], v[j]) + inter[j]
    for j in heads:
        st_scr[j] = st[j] * jnp.exp(blast[j]) + upd[j]


def gla_scan(p, wg, bg, nheads, dk, dv, nc, nl, hpb=2):
    bsz, l, _ = p.shape
    idx = lambda n, d: _chunk_index(n, d, nc, nl)
    kq = nheads * dk
    ngrp = nheads // hpb
    return pl.pallas_call(
        functools.partial(_gla_kernel, dk=dk, dv=dv, hpb=hpb),
        grid=(bsz, 2, ngrp, nc + nl),
        in_specs=[pl.BlockSpec((1, CHUNK, hpb * dk), lambda b, d, h, n: (b, idx(n, d), h)),
                  pl.BlockSpec((1, CHUNK, hpb * dk), lambda b, d, h, n: (b, idx(n, d), ngrp + h)),
                  pl.BlockSpec((1, CHUNK, hpb * dv), lambda b, d, h, n: (b, idx(n, d), 2 * kq // (hpb * dv) + h)),
                  pl.BlockSpec((1, CHUNK, 128), lambda b, d, h, n: (b, idx(n, d), (2 * kq + 2 * nheads * dv) // 128)),
                  pl.BlockSpec((1, GLA_GATE_RANK, hpb * dk), lambda b, d, h, n: (d, 0, h)),
                  pl.BlockSpec((1, 1, hpb * dk), lambda b, d, h, n: (d, 0, h))],
        out_specs=pl.BlockSpec((1, 1, CHUNK, hpb * dv), lambda b, d, h, n: (d, b, idx(n, d), h)),
        out_shape=jax.ShapeDtypeStruct((2, bsz, l, nheads * dv), F32),
        scratch_shapes=[pltpu.VMEM((hpb, dv, dk), F32)],
        compiler_params=pltpu.CompilerParams(
            dimension_semantics=("parallel", "parallel", "parallel", "arbitrary"), vmem_limit_bytes=VMEM_LIMIT),
        name="gla_scan",
    )(p, p, p, p, wg, bg)


def scan_gate_tables(val, logdec, nc, nl, per_group, ndir=2):
    bsz, l, _, h = val.shape
    nch = l // CHUNK
    ld = logdec.reshape(bsz, nch, CHUNK, ndir, h)
    cums = [jnp.cumsum(ld[:, :, :, 0], axis=2)]
    if ndir == 2:
        cums.append(jnp.cumsum(ld[:, :, ::-1, 1], axis=2)[:, :, ::-1])
    cum = jnp.stack(cums, axis=3)
    tot = jnp.broadcast_to(jnp.sum(ld, axis=2, keepdims=True), cum.shape)
    ngrp = h // per_group
    col = jnp.stack([val.reshape(bsz, nch, CHUNK, ndir, h), cum, tot], axis=-1)
    col = col.reshape(bsz, l, ndir, ngrp, per_group * 3).transpose(0, 2, 3, 1, 4)
    col = jnp.pad(col, ((0, 0),) * 4 + ((0, 128 - per_group * 3),))
    rowt = cum.reshape(bsz, nch, CHUNK, ndir, ngrp, per_group).transpose(0, 3, 4, 1, 5, 2)
    rowt = jnp.pad(rowt, ((0, 0),) * 4 + ((0, 8 - per_group), (0, 0)))
    return col, rowt


def tri_masks():
    i = jnp.arange(CHUNK)
    return i[:, None] >= i[None, :], i[:, None] > i[None, :]


def chunk_heads(t):
    b, l, h, d = t.shape
    return t.reshape(b, l // CHUNK, CHUNK, h, d).transpose(0, 3, 1, 2, 4)


def chunk_scalar(t):
    b, l, h = t.shape
    return t.reshape(b, l // CHUNK, CHUNK, h).transpose(0, 3, 1, 2)


def unchunk_heads(o):
    n, b, h, c, d = o.shape
    return o.transpose(1, 0, 3, 2, 4).reshape(b, n * c, h, d)


def chunks_first(ts):
    return tuple(jnp.moveaxis(t, 2, 0) for t in ts)


def gdn_core(q, k, v, g, beta, s0):
    q, k, v = (chunk_heads(t.astype(F32)) for t in (q, k, v))
    g = jnp.cumsum(chunk_scalar(g.astype(F32)), axis=-1)
    beta = chunk_scalar(beta.astype(F32))
    incl, strict = tri_masks()
    decay = jnp.where(incl, jnp.exp(jnp.where(incl, g[..., :, None] - g[..., None, :], 0.0)), 0.0)
    kb = k * beta[..., None]
    a = jnp.where(strict, jnp.einsum('bhncd,bhnsd->bhncs', kb, k) * decay, 0.0) + jnp.eye(CHUNK, dtype=F32)
    eg = jnp.exp(g)
    w = lax.linalg.triangular_solve(a, kb * eg[..., None], left_side=True, lower=True, unit_diagonal=True)
    u = lax.linalg.triangular_solve(a, v * beta[..., None], left_side=True, lower=True, unit_diagonal=True)
    attn = jnp.einsum('bhncd,bhnsd->bhncs', q, k) * decay
    g_last = g[..., -1]
    xs = chunks_first((q * eg[..., None], w, u, attn, k * jnp.exp(g_last[..., None] - g)[..., None], jnp.exp(g_last)))

    def step(S, inp):
        q_c, w_c, u_c, a_c, k_c, dec = inp
        v_new = u_c - jnp.einsum('bhcd,bhde->bhce', w_c, S)
        o = jnp.einsum('bhcd,bhde->bhce', q_c, S) + jnp.einsum('bhcs,bhse->bhce', a_c, v_new)
        S = S * dec[..., None, None] + jnp.einsum('bhcd,bhce->bhde', k_c, v_new)
        return S, o

    S, o = lax.scan(step, s0, xs)
    return unchunk_heads(o), S


def even_mixer(hc, hl, w_in, gdn_conv_w, gdn_a_log, gdn_dt_bias, gdn_norm_w, ssd_conv_w, ssd_conv_b,
               ssd_a_log, ssd_dt_bias, ssd_d, ssd_norm_w, w_out, need_ctx):
    def prep(h):
        bsz, l, _ = h.shape
        qkv, z_a, gates, z_b, xbc, dt_raw = split_cols(proj(h, w_in, 640), EVEN_SPLITS)
        qkv = jax.nn.silu(dwconv(qkv, gdn_conv_w))
        q, k, v = split_cols(qkv, (GDN_QK_DIM, GDN_QK_DIM, GDN_V_DIM))
        q = l2norm(q.reshape(bsz, l, GDN_HEADS, GDN_DK)) * GDN_DK ** -0.5
        k = l2norm(k.reshape(bsz, l, GDN_HEADS, GDN_DK))
        v = v.reshape(bsz, l, GDN_HEADS, GDN_DV)
        gates = gates.astype(F32).reshape(bsz, l, 4, GDN_HEADS)
        beta = jax.nn.sigmoid(gates[:, :, 0:2])
        g = -jnp.exp(gdn_a_log.astype(F32)) * jax.nn.softplus(gates[:, :, 2:4] + gdn_dt_bias.astype(F32))
        gdn_f = (q, k, v, g[:, :, 0], beta[:, :, 0])
        gdn_b = (q, k, v, g[:, :, 1], beta[:, :, 1])
        xbc = jax.nn.silu(dwconv(xbc, ssd_conv_w) + ssd_conv_b)
        xs, bm, cm = split_cols(xbc, (SSD_INNER, SSD_GROUPS * SSD_STATE, SSD_GROUPS * SSD_STATE))
        dt = jax.nn.softplus(dt_raw.astype(F32).reshape(bsz, l, 2, SSD_HEADS) + ssd_dt_bias.astype(F32))
        dA = -dt * jnp.exp(ssd_a_log.astype(F32))
        return gdn_f, gdn_b, (xs, bm, cm, dt, dA), (z_a, z_b)

    def finish(o_gdn, o_ssd, z_a, z_b, xs, dtype):
        bsz, l = o_gdn.shape[:2]
        xs = xs.reshape(bsz, l, SSD_HEADS, SSD_HEADDIM)
        y_a = rms_normalize(o_gdn) * gdn_norm_w.astype(F32) * jax.nn.silu(z_a.astype(F32).reshape(bsz, l, GDN_HEADS, GDN_DV))
        y_b = (o_ssd + ssd_d.astype(F32)[:, None] * xs.astype(F32)).reshape(bsz, l, SSD_INNER) * jax.nn.silu(z_b.astype(F32))
        y_b = rms_normalize(y_b.reshape(bsz, l, SSD_GROUPS, -1)).reshape(bsz, l, SSD_INNER) * ssd_norm_w.astype(F32)
        y = jnp.concatenate([y_a.reshape(bsz, l, GDN_V_DIM), y_b], axis=-1)
        return proj(y.astype(dtype), w_out, 512)

    pc, pl_ = prep(hc), prep(hl)
    bsz = hl.shape[0]
    rev = lambda ts: tuple(t[:, ::-1] for t in ts)
    s0 = jnp.zeros((bsz, GDN_HEADS, GDN_DK, GDN_DV), F32)
    o_cb, s_b = gdn_core(*rev(pc[1]), s0)
    o_lb, _ = gdn_core(*rev(pl_[1]), s_b)
    cat = lambda tc, tl: jnp.concatenate([tc, tl], axis=1)
    lcat = CTX_LEN + SEQ
    gcol, grow = scan_gate_tables(cat(pc[0][4], pl_[0][4])[:, :, None, :], cat(pc[0][3], pl_[0][3])[:, :, None, :],
                                  NC_CTX, NC_LAT, 8, ndir=1)
    o_f = gdn_scan(cat(pc[0][0], pl_[0][0]).reshape(bsz, lcat, GDN_QK_DIM), cat(pc[0][1], pl_[0][1]).reshape(bsz, lcat, GDN_QK_DIM),
                   cat(pc[0][2], pl_[0][2]).reshape(bsz, lcat, GDN_V_DIM), gcol, grow, NC_CTX, NC_LAT, 8, ndir=1)[0]
    o_f = o_f.reshape(bsz, lcat, GDN_HEADS, GDN_DV)
    o_gc = o_f[:, :CTX_LEN] + o_cb[:, ::-1]
    o_gl = o_f[:, CTX_LEN:] + o_lb[:, ::-1]
    xs, bm, cm, dt, dA = (jnp.concatenate([tc, tl], axis=1) for tc, tl in zip(pc[2], pl_[2]))
    acol, arow = scan_gate_tables(dt, dA, NC_CTX, NC_LAT, 8)
    o_ssd = ssd_scan(xs, bm, cm, acol, arow, NC_CTX, NC_LAT)
    o_ssd = (o_ssd[0] + o_ssd[1]).reshape(bsz, CTX_LEN + SEQ, SSD_HEADS, SSD_HEADDIM)
    y_l = finish(o_gl, o_ssd[:, CTX_LEN:], *pl_[3], pl_[2][0], hl.dtype)
    y_c = finish(o_gc, o_ssd[:, :CTX_LEN], *pc[3], pc[2][0], hc.dtype) if need_ctx else None
    return y_c, y_l


def odd_mixer(a, w_in, gla_w_gate, gla_b_gate, gla_norm_w, w_out):
    bsz, l, _ = a.shape
    p = proj(a, jnp.pad(w_in, ((0, 0), (0, ODD_IN_PAD - ODD_IN))), 896)
    o = gla_scan(p, gla_w_gate, gla_b_gate.reshape(2, 1, GLA_HEADS * GLA_DK), GLA_HEADS, GLA_DK, GLA_DV, NC_CTX, NC_LAT,
                 hpb=GLA_HEADS)
    o = (o[0] + o[1]).reshape(bsz, l, GLA_HEADS, GLA_DV)
    r = p[..., 2 * GLA_HEADS * GLA_DK + ODD_MIX:2 * GLA_HEADS * GLA_DK + 2 * ODD_MIX]
    y = rms_normalize(o) * gla_norm_w.astype(F32) * jax.nn.silu(r.astype(F32).reshape(bsz, l, GLA_HEADS, GLA_DV))
    return proj(y.reshape(bsz, l, ODD_MIX), w_out, 512)


def _ce(r, i, l):
    hi = jnp.maximum(r[i], r[l])
    lo = jnp.minimum(r[i], r[l])
    r[i], r[l] = hi, lo


def _bitonic_sort_desc(r):
    n = len(r)
    k = 2
    while k <= n:
        j = k // 2
        while j >= 1:
            for i in range(n):
                l = i ^ j
                if l > i:
                    if (i & k) == 0:
                        _ce(r, i, l)
                    else:
                        _ce(r, l, i)
            j //= 2
        k *= 2
    return r


def _bitonic_merge_desc(r):
    n = len(r)
    j = n // 2
    while j >= 1:
        for i in range(n):
            l = i ^ j
            if l > i:
                _ce(r, i, l)
        j //= 2
    return r


_PEER_CANDS = [(a, b) for a in range(PEER_TOPK + 1) for b in range(PEER_TOPK + 1) if (a + 1) * (b + 1) <= PEER_TOPK + 1]


def _peer_score_kernel(xn_ref, wq_ref, sk_ref, s2_ref, e2z_ref, thr_ref, e1_ref, s1_scr):
    tm = xn_ref.shape[0]
    nk = PEER_NKEYS
    q = jnp.dot(xn_ref[...], wq_ref[...], preferred_element_type=F32).astype(BF16)
    sub = lax.broadcasted_iota(jnp.int32, (8, tm), 0)
    packed = [[None] * (PEER_TOPK + 1), [None] * (PEER_TOPK + 1)]
    for h in range(PEER_HEADS):
        for p in range(2):
            hp = 2 * h + p
            s = lax.dot_general(sk_ref[hp], q[:, hp * (PEER_DKEY // 2):(hp + 1) * (PEER_DKEY // 2)],
                                (((1,), (1,)), ((), ())), preferred_element_type=F32)
            if p == 0:
                s1_scr[h] = s
            else:
                s2_ref[h] = s
            r = _bitonic_sort_desc([s[8 * g:8 * g + 8, :] for g in range(nk // 8)])
            nxt = None
            for shift in (4, 2, 1):
                other = [pltpu.roll(r[PEER_TOPK - 1 - i], shift, 0) for i in range(PEER_TOPK)]
                drop = functools.reduce(jnp.maximum, [jnp.minimum(r[i], other[i]) for i in range(PEER_TOPK)])
                nxt = drop if nxt is None else jnp.maximum(drop, jnp.maximum(nxt, pltpu.roll(nxt, shift, 0)))
                r = _bitonic_merge_desc([jnp.maximum(r[i], other[i]) for i in range(PEER_TOPK)])
            r = r + [nxt]
            for a in range(PEER_TOPK + 1):
                packed[p][a] = r[a] if h == 0 else jnp.where(sub == h, r[a], packed[p][a])
    top = [jnp.full((8, tm), -jnp.inf, F32) for _ in range(PEER_TOPK + 1)]
    for a, b in _PEER_CANDS:
        x = packed[0][a] + packed[1][b]
        for pos in range(PEER_TOPK + 1):
            hi = jnp.maximum(top[pos], x)
            x = jnp.minimum(top[pos], x)
            top[pos] = hi
    tau = 0.5 * (top[PEER_TOPK - 1] + top[PEER_TOPK])
    z = jnp.exp(top[0] - top[0])
    for pos in range(1, PEER_TOPK):
        z = z + jnp.exp(top[pos] - top[0])
    rz = 1.0 / z
    m1, m2 = packed[0][0], packed[1][0]
    for h in range(PEER_HEADS):
        s1 = s1_scr[h]
        s2 = s2_ref[h]
        thr_ref[h] = tau[h:h + 1, :] - s1
        e1_ref[h] = jnp.exp(s1 - m1[h:h + 1, :])
        e2z_ref[h] = jnp.exp(s2 - m2[h:h + 1, :]) * rz[h:h + 1, :]


def peer_scores(xn, wq, sk, tm):
    t, d = xn.shape
    tab = jax.ShapeDtypeStruct((PEER_HEADS, PEER_NKEYS, t), F32)
    tab_spec = pl.BlockSpec((PEER_HEADS, PEER_NKEYS, tm), lambda i: (0, 0, i))
    return pl.pallas_call(
        _peer_score_kernel,
        grid=(t // tm,),
        in_specs=[pl.BlockSpec((tm, d), lambda i: (i, 0)),
                  pl.BlockSpec(wq.shape, lambda i: (0, 0)),
                  pl.BlockSpec(sk.shape, lambda i: (0, 0, 0))],
        out_specs=[tab_spec] * 4,
        out_shape=[tab] * 4,
        scratch_shapes=[pltpu.VMEM((PEER_HEADS, PEER_NKEYS, tm), F32)],
        compiler_params=pltpu.CompilerParams(dimension_semantics=("parallel",), vmem_limit_bytes=VMEM_LIMIT),
        name="peer_scores",
    )(xn, wq, sk)


def _gelu(x):
    return 0.5 * x * (1.0 + lax.erf(x * (2.0 ** -0.5)))


def _peer_main_kernel(xnT_ref, u_ref, v_ref, s2_ref, e2z_ref, thr_ref, e1_ref, o_ref, pre_scr, coef_scr):
    te, tm = pre_scr.shape
    e = pl.program_id(1)

    @pl.when(e == 0)
    def _():
        o_ref[...] = jnp.zeros_like(o_ref)

    half = tm // 2
    for hf in range(2):
        hs = pl.ds(hf * half, half)
        pre_scr[:, hs] = jnp.dot(u_ref[...], xnT_ref[:, hs], preferred_element_type=F32)
    for hf in range(2):
        hs = pl.ds(hf * half, half)
        for c in range(hf * half // 128, (hf + 1) * half // 128):
            cs = pl.ds(c * 128, 128)
            for ii in range(te // PEER_NKEYS):
                rs = pl.ds(ii * PEER_NKEYS, PEER_NKEYS)
                w = jnp.zeros((PEER_NKEYS, 128), F32)
                for h in range(PEER_HEADS):
                    thr = thr_ref[h, pl.ds(ii, 1), cs]
                    e1 = e1_ref[h, pl.ds(ii, 1), cs]
                    w = w + jnp.where(s2_ref[h, :, cs] >= thr, e2z_ref[h, :, cs], 0.0) * e1
                coef_scr[rs, cs] = (w * _gelu(pre_scr[rs, cs])).astype(BF16)
        o_ref[hs, :] += lax.dot_general(coef_scr[:, hs], v_ref[...], (((0,), (0,)), ((), ())),
                                        preferred_element_type=F32)


def peer_main(xnT, ub, vb, s2, e2z, thr, e1, tm, te):
    d, t = xnT.shape
    ne = ub.shape[0]
    tab_spec = pl.BlockSpec((PEER_HEADS, PEER_NKEYS, tm), lambda i, e: (0, 0, i))
    row_spec = pl.BlockSpec((PEER_HEADS, te // PEER_NKEYS, tm), lambda i, e: (0, e, i))
    return pl.pallas_call(
        _peer_main_kernel,
        grid=(t // tm, ne // te),
        in_specs=[pl.BlockSpec((d, tm), lambda i, e: (0, i)),
                  pl.BlockSpec((te, d), lambda i, e: (e, 0)),
                  pl.BlockSpec((te, d), lambda i, e: (e, 0)),
                  tab_spec, tab_spec, row_spec, row_spec],
        out_specs=pl.BlockSpec((tm, d), lambda i, e: (i, 0)),
        out_shape=jax.ShapeDtypeStruct((t, d), F32),
        scratch_shapes=[pltpu.VMEM((te, tm), F32), pltpu.VMEM((te, tm), BF16)],
        compiler_params=pltpu.CompilerParams(dimension_semantics=("parallel", "arbitrary"),
                                             vmem_limit_bytes=56 * 1024 * 1024),
        name="peer_main",
    )(xnT, ub, vb, s2, e2z, thr, e1)


def peer_ffn(h, wq, sk, ub, vb):
    bsz, l, d = h.shape
    xn = h.reshape(bsz * l, d).astype(BF16)
    s2, e2z, thr, e1 = peer_scores(xn, wq, sk, 256)
    return peer_main(xn.T, ub, vb, s2, e2z, thr, e1, 512, 1024).reshape(bsz, l, d)


def kernel(x, c, ctx, c_ctx, ada_w, ada_b, norm1_w, norm2_w, ev_w_in, gdn_conv_w, gdn_a_log, gdn_dt_bias,
           gdn_norm_w, ssd_conv_w, ssd_conv_b, ssd_a_log, ssd_dt_bias, ssd_d, ssd_norm_w, ev_w_out,
           od_w_in, gla_w_gate, gla_b_gate, gla_norm_w, od_w_out, peer_w_q, peer_subkeys, peer_u, peer_v,
           final_norm_w):
    h, hc = x, ctx
    sc_lat = jax.nn.silu(c)
    sc_ctx = jax.nn.silu(c_ctx)
    for i in range(DEPTH):
        need_ctx = i < DEPTH - 1
        j = i // 2
        shift1, scale1, gate1, shift2, scale2, gate2 = (m[:, None, :] for m in jnp.split(sc_lat @ ada_w[i] + ada_b[i], 6, axis=-1))
        cshift1, cscale1, cgate1, cshift2, cscale2, cgate2 = jnp.split(sc_ctx @ ada_w[i] + ada_b[i], 6, axis=-1)
        a_l = rmsnorm(h, norm1_w[i]) * (1 + scale1) + shift1
        a_c = rmsnorm(hc, norm1_w[i]) * (1 + cscale1) + cshift1
        if i % 2 == 0:
            y_c, y_l = even_mixer(a_c, a_l, ev_w_in[j], gdn_conv_w[j], gdn_a_log[j], gdn_dt_bias[j], gdn_norm_w[j],
                                  ssd_conv_w[j], ssd_conv_b[j], ssd_a_log[j], ssd_dt_bias[j], ssd_d[j], ssd_norm_w[j],
                                  ev_w_out[j], need_ctx)
        else:
            y = odd_mixer(jnp.concatenate([a_c, to_col_major(a_l)], axis=1), od_w_in[j], gla_w_gate[j], gla_b_gate[j],
                          gla_norm_w[j], od_w_out[j])
            y_c, y_l = y[:, :CTX_LEN], to_row_major(y[:, CTX_LEN:])
        peer_w = (peer_w_q[i].astype(BF16),
                  peer_subkeys[i].reshape(2 * PEER_HEADS, PEER_NKEYS, PEER_DKEY // 2).astype(BF16),
                  peer_u[i].astype(BF16), peer_v[i].astype(BF16))
        h = h + gate1 * y_l
        h = h + gate2 * peer_ffn(rmsnorm(h, norm2_w[i]) * (1 + scale2) + shift2, *peer_w)
        if need_ctx:
            hc = hc + cgate1 * y_c
            hc = hc + cgate2 * peer_ffn(rmsnorm(hc, norm2_w[i]) * (1 + cscale2) + cshift2, *peer_w)
    return rmsnorm(h, final_norm_w)
```
